```python
import math
import jax, jax.numpy as jnp
from jax import lax
import numpy as np

D_MODEL = 1024
BATCH = 8
SEQ = 8192
DEPTH = 1
DEC_BATCH = 32
DEC_SEQ = 2048
PAST_LEN = 128

N_META = 16
ATTN_WIDTH = 512
HYENA_WIDTH = D_MODEL - ATTN_WIDTH
N_HEADS = 4
QK_NOPE = 128
QK_ROPE = 64
V_HEAD = ATTN_WIDTH // N_HEADS
Q_LORA = 256
KV_LORA = 128
ROPE_BASE = 10000.0
Q_BLOCK = 128
HYENA_ORDER = 2
SHORT_CONV = 3
FILTER_EMB = 33
FILTER_BANDS = (FILTER_EMB - 1) // 2
FILTER_HIDDEN = 64
N_FILTERS = HYENA_ORDER * 2 * HYENA_WIDTH
DECAY_TARGET = 1e-2
FAST_DECAY_PCT = 0.3
SLOW_DECAY_PCT = 1.5
D_FF = 4 * D_MODEL
NORM_EPS = 1e-6
IN_COLS = Q_LORA + KV_LORA + QK_ROPE + (HYENA_ORDER + 1) * HYENA_WIDTH

kernel_name = "hymba_mla_hyena_encoder"


def rms_norm(x, g):
    xf = x.astype(jnp.float32)
    y = xf * lax.rsqrt(jnp.mean(xf * xf, axis=-1, keepdims=True) + NORM_EPS)
    return (y * g.astype(jnp.float32)).astype(x.dtype)


def rope_tables(L):
    inv = 1.0 / (ROPE_BASE ** (jnp.arange(0, QK_ROPE, 2, dtype=jnp.float32) / QK_ROPE))
    ang = jnp.arange(L, dtype=jnp.float32)[:, None] * inv[None, :]
    ang = jnp.concatenate([ang, ang], axis=-1)
    return jnp.cos(ang), jnp.sin(ang)


def apply_rope(x, cos, sin):
    xf = x.astype(jnp.float32)
    x1, x2 = jnp.split(xf, 2, axis=-1)
    rot = jnp.concatenate([-x2, x1], axis=-1)
    return (xf * cos + rot * sin).astype(x.dtype)


def mla(c_q, c_kv, k_rope, q_norm_g, w_uq, kv_norm_g, w_ukv):
    B, L, _ = c_q.shape
    q = (rms_norm(c_q, q_norm_g) @ w_uq).reshape(B, L, N_HEADS, QK_NOPE + QK_ROPE)
    q_nope, q_rope = q[..., :QK_NOPE], q[..., QK_NOPE:]
    kv = (rms_norm(c_kv, kv_norm_g) @ w_ukv).reshape(B, L, N_HEADS, QK_NOPE + V_HEAD)
    k_nope, v = kv[..., :QK_NOPE], kv[..., QK_NOPE:]
    cos, sin = rope_tables(L)
    q_rope = apply_rope(q_rope, cos[None, :, None, :], sin[None, :, None, :])
    k_rope = apply_rope(k_rope, cos[None], sin[None])
    scale = (QK_NOPE + QK_ROPE) ** -0.5
    n_blk = -(-L // Q_BLOCK)
    pad = n_blk * Q_BLOCK - L

    def to_blocks(t):
        t = jnp.pad(t, ((0, 0), (0, pad), (0, 0), (0, 0)))
        return t.reshape(B, n_blk, Q_BLOCK, N_HEADS, t.shape[-1]).transpose(1, 0, 2, 3, 4)

    def attend(blk):
        qn, qr = blk
        s = (jnp.einsum('bqhd,bkhd->bhqk', qn, k_nope, preferred_element_type=jnp.float32)
             + jnp.einsum('bqhr,bkr->bhqk', qr, k_rope, preferred_element_type=jnp.float32))
        p = jax.nn.softmax(s * scale, axis=-1)
        return jnp.einsum('bhqk,bkhd->bqhd', p.astype(v.dtype), v)

    o = lax.map(attend, (to_blocks(q_nope), to_blocks(q_rope)))
    o = o.transpose(1, 0, 2, 3, 4).reshape(B, n_blk * Q_BLOCK, N_HEADS * V_HEAD)
    return o[:, :L]


def hyena_filters(L, f_w1, f_b1, f_freq, f_w2, f_b2, f_w3, f_decay):
    f32 = jnp.float32
    t = jnp.linspace(0.0, 1.0, L, dtype=f32)[:, None]
    w = 2.0 * math.pi * jnp.arange(L, dtype=f32)[:, None] / L
    f = jnp.linspace(1e-4, FILTER_BANDS - 1, FILTER_BANDS, dtype=f32)[None, :]
    feats = jnp.concatenate([t, jnp.cos(f * w), -jnp.sin(f * w)], axis=-1)
    freq = f_freq.astype(f32)
    h = jnp.sin(freq * (feats @ f_w1.astype(f32) + f_b1.astype(f32)))
    h = jnp.sin(freq * (h @ f_w2.astype(f32) + f_b2.astype(f32)))
    h = h @ f_w3.astype(f32)
    h = h * jnp.exp(-t * jnp.abs(f_decay.astype(f32)))
    h = h.reshape(L, HYENA_ORDER, 2, HYENA_WIDTH)
    h_fwd, h_bwd = h[:, :, 0], h[:, :, 1]
    zero = jnp.zeros_like(h_fwd[:1])
    k = jnp.concatenate([h_fwd, zero, h_bwd[1:][::-1]], axis=0)
    return jnp.fft.rfft(k, axis=0)


def hyena(u, conv_w, filt_fft, hy_bias):
    B, L, _ = u.shape
    C = HYENA_WIDTH
    up = jnp.pad(u, ((0, 0), (1, 1), (0, 0)))
    u = up[:, :-2] * conv_w[0] + up[:, 1:-1] * conv_w[1] + up[:, 2:] * conv_w[2]
    z = u[..., HYENA_ORDER * C:].astype(jnp.float32)
    for n in range(HYENA_ORDER):
        y = jnp.fft.irfft(jnp.fft.rfft(z, n=2 * L, axis=1) * filt_fft[None, :, n], n=2 * L, axis=1)[:, :L]
        z = u[..., n * C:(n + 1) * C].astype(jnp.float32) * (y + z * hy_bias[n].astype(jnp.float32))
    return z.astype(u.dtype)


def trunk(x, meta_tokens, pre_mix_g, w_in, q_norm_g, w_uq, kv_norm_g, w_ukv, conv_w,
          f_w1, f_b1, f_freq, f_w2, f_b2, f_w3, f_decay, hy_bias, attn_out_g, hy_out_g,
          w_o, post_mix_g, pre_mlp_g, w_ff1, w_ff2, post_mlp_g):
    B = x.shape[0]
    meta = jnp.broadcast_to(meta_tokens.astype(x.dtype)[None], (B, N_META, D_MODEL))
    h = jnp.concatenate([meta, x], axis=1)
    L = h.shape[1]
    s1 = Q_LORA
    s2 = Q_LORA + KV_LORA
    s3 = Q_LORA + KV_LORA + QK_ROPE
    for i in range(DEPTH):
        a = rms_norm(h, pre_mix_g[i])
        p = a @ w_in[i]
        c_q, c_kv, k_rope, u = p[..., :s1], p[..., s1:s2], p[..., s2:s3], p[..., s3:]
        o_attn = mla(c_q, c_kv, k_rope, q_norm_g[i], w_uq[i], kv_norm_g[i], w_ukv[i])
        filt = hyena_filters(L, f_w1[i], f_b1[i], f_freq[i], f_w2[i], f_b2[i], f_w3[i], f_decay[i])
        o_hy = hyena(u, conv_w[i], filt, hy_bias[i])
        mix = jnp.concatenate([rms_norm(o_attn, attn_out_g[i]), rms_norm(o_hy, hy_out_g[i])], axis=-1) @ w_o[i]
        h = h + rms_norm(mix, post_mix_g[i])
        m = rms_norm(h, pre_mlp_g[i]) @ w_ff1[i]
        m = jnp.square(jax.nn.relu(m)) @ w_ff2[i]
        h = h + rms_norm(m, post_mlp_g[i])
    return h[:, N_META:]


def setup_inputs(seed: int = 0) -> dict:
    key = jax.random.key(seed)
    ks = jax.random.split(key, 32)
    f32 = jnp.float32

    def nrm(k, shape, scale):
        return jax.random.normal(k, shape, f32) * scale

    def gain(k, shape):
        return 1.0 + 0.02 * jax.random.normal(k, shape, f32)

    min_decay = math.log(DECAY_TARGET) / SLOW_DECAY_PCT
    max_decay = math.log(DECAY_TARGET) / FAST_DECAY_PCT
    base = jnp.tile(jnp.abs(jnp.linspace(min_decay, max_decay, HYENA_WIDTH, dtype=f32)), HYENA_ORDER * 2)
    f_decay = base[None, :] * (1.0 + 0.05 * jax.random.normal(ks[15], (DEPTH, N_FILTERS), f32))
    return {
        "x_prompt": nrm(ks[0], (BATCH, SEQ, D_MODEL), 1.0),
        "x_sample": nrm(ks[1], (DEC_BATCH, DEC_SEQ, D_MODEL), 1.0),
        "meta_tokens": nrm(ks[2], (N_META, D_MODEL), 1.0),
        "pre_mix_g": gain(ks[3], (DEPTH, D_MODEL)),
        "w_in": nrm(ks[4], (DEPTH, D_MODEL, IN_COLS), D_MODEL ** -0.5),
        "q_norm_g": gain(ks[5], (DEPTH, Q_LORA)),
        "w_uq": nrm(ks[6], (DEPTH, Q_LORA, N_HEADS * (QK_NOPE + QK_ROPE)), Q_LORA ** -0.5),
        "kv_norm_g": gain(ks[7], (DEPTH, KV_LORA)),
        "w_ukv": nrm(ks[8], (DEPTH, KV_LORA, N_HEADS * (QK_NOPE + V_HEAD)), KV_LORA ** -0.5),
        "conv_w": nrm(ks[9], (DEPTH, SHORT_CONV, (HYENA_ORDER + 1) * HYENA_WIDTH), SHORT_CONV ** -0.5),
        "f_w1": nrm(ks[10], (DEPTH, FILTER_EMB, FILTER_HIDDEN), FILTER_EMB ** -0.5),
        "f_b1": nrm(ks[11], (DEPTH, FILTER_HIDDEN), 0.02),
        "f_freq": gain(ks[12], (DEPTH, FILTER_HIDDEN)),
        "f_w2": nrm(ks[13], (DEPTH, FILTER_HIDDEN, FILTER_HIDDEN), FILTER_HIDDEN ** -0.5),
        "f_b2": nrm(ks[14], (DEPTH, FILTER_HIDDEN), 0.02),
        "f_w3": nrm(ks[16], (DEPTH, FILTER_HIDDEN, N_FILTERS), FILTER_HIDDEN ** -0.5),
        "f_decay": f_decay,
        "hy_bias": nrm(ks[17], (DEPTH, HYENA_ORDER, HYENA_WIDTH), 1.0),
        "attn_out_g": gain(ks[18], (DEPTH, ATTN_WIDTH)),
        "hy_out_g": gain(ks[19], (DEPTH, HYENA_WIDTH)),
        "w_o": nrm(ks[20], (DEPTH, D_MODEL, D_MODEL), D_MODEL ** -0.5),
        "post_mix_g": gain(ks[21], (DEPTH, D_MODEL)),
        "pre_mlp_g": gain(ks[22], (DEPTH, D_MODEL)),
        "w_ff1": nrm(ks[23], (DEPTH, D_MODEL, D_FF), D_MODEL ** -0.5),
        "w_ff2": nrm(ks[24], (DEPTH, D_FF, D_MODEL), D_FF ** -0.5),
        "post_mlp_g": gain(ks[25], (DEPTH, D_MODEL)),
    }


def reference(x_prompt, x_sample, meta_tokens, pre_mix_g, w_in, q_norm_g, w_uq, kv_norm_g, w_ukv,
              conv_w, f_w1, f_b1, f_freq, f_w2, f_b2, f_w3, f_decay, hy_bias, attn_out_g, hy_out_g,
              w_o, post_mix_g, pre_mlp_g, w_ff1, w_ff2, post_mlp_g):
    y_prompt = trunk(x_prompt, meta_tokens, pre_mix_g, w_in, q_norm_g, w_uq, kv_norm_g, w_ukv, conv_w,
                     f_w1, f_b1, f_freq, f_w2, f_b2, f_w3, f_decay, hy_bias, attn_out_g, hy_out_g,
                     w_o, post_mix_g, pre_mlp_g, w_ff1, w_ff2, post_mlp_g)
    y_sample = trunk(x_sample, meta_tokens, pre_mix_g, w_in, q_norm_g, w_uq, kv_norm_g, w_ukv, conv_w,
                     f_w1, f_b1, f_freq, f_w2, f_b2, f_w3, f_decay, hy_bias, attn_out_g, hy_out_g,
                     w_o, post_mix_g, pre_mlp_g, w_ff1, w_ff2, post_mlp_g)
    return (y_prompt, y_sample)
```

```python
import functools
import math

import numpy as np
import jax
import jax.numpy as jnp
from jax import lax
from jax.experimental import pallas as pl
from jax.experimental.pallas import tpu as pltpu

F32 = jnp.float32
BF16 = jnp.bfloat16

D_MODEL = 1024
N_META = 16
ATTN_WIDTH = 512
HYENA_WIDTH = 512
N_HEADS = 4
QK_NOPE = 128
QK_ROPE = 64
V_HEAD = 128
Q_LORA = 256
KV_LORA = 128
ROPE_BASE = 10000.0
FILTER_EMB = 33
FILTER_BANDS = 16
HYENA_ORDER = 2
D_FF = 4096
NORM_EPS = 1e-6

LANE = 128
SUBLANE = 8
BF16_ROWS = 16
VMEM_LIMIT = 56 * 1024 * 1024

Q_SCALE = (QK_NOPE + QK_ROPE) ** -0.5 * math.log2(math.e)


def _round_up(x, m):
    return -(-x // m) * m


def _pick_tile(n, cap, mult):
    best = None
    for d in range(mult, min(n, cap) + 1, mult):
        if n % d == 0:
            best = d
    return n if best is None else best


def _rms(x, g):
    return x * lax.rsqrt(jnp.mean(x * x, axis=-1, keepdims=True) + NORM_EPS) * g


def _const_spec(shape):
    nd = len(shape)
    return pl.BlockSpec(shape, lambda *_: (0,) * nd, pipeline_mode=pl.Buffered(1))


class _Geom:
    def __init__(self, L):
        self.L = L
        self.n1h = -(-L // LANE)
        n1 = -(-(2 * L - 1) // LANE)
        self.n1 = n1 + (n1 % 2)
        self.N = self.n1 * LANE
        self.k1 = self.n1 // 2 + 1
        self.k1p = _round_up(self.k1, SUBLANE)
        self.ka = 2 * self.k1p
        self.n1hp = _round_up(self.n1h, BF16_ROWS)
        self.Lp = self.n1h * LANE
        self.zrows = self.n1hp * LANE


@functools.lru_cache(maxsize=None)
def _tables(L):
    g = _Geom(L)
    n2 = np.arange(LANE)[:, None, None]
    k1 = np.arange(g.k1)[None, :, None]
    n1 = np.arange(g.n1hp)[None, None, :]
    n = n1 * LANE + n2
    ang = 2.0 * np.pi * ((k1 * n) % g.N).astype(np.float64) / g.N
    valid = (n1 < g.n1h)
    fa = np.zeros((LANE, g.ka, g.n1hp), np.float32)
    fa[:, : g.k1, :] = np.cos(ang) * valid
    fa[:, g.k1p : g.k1p + g.k1, :] = -np.sin(ang) * valid
    kk = np.arange(LANE)
    th = 2.0 * np.pi * ((kk[:, None] * kk[None, :]) % LANE) / LANE
    c, s = np.cos(th), np.sin(th)
    fbf = np.block([[c, s], [-s, c]]).astype(np.float32)
    fbi = np.block([[c, -s], [s, c]]).astype(np.float32)
    inv = 1.0 / (ROPE_BASE ** (np.arange(0, QK_ROPE, 2, dtype=np.float64) / QK_ROPE))
    pos = np.arange(g.Lp, dtype=np.float64)[:, None] * inv[None, :]
    pos = np.concatenate([pos, pos], axis=-1)
    ct = np.zeros((g.Lp, LANE), np.float32)
    st = np.zeros((g.Lp, LANE), np.float32)
    ct[:, :QK_ROPE] = np.cos(pos)
    st[:, :QK_ROPE] = np.sin(pos)
    t = np.linspace(0.0, 1.0, L)[:, None]
    w = 2.0 * np.pi * np.arange(L, dtype=np.float64)[:, None] / L
    f = np.linspace(1e-4, FILTER_BANDS - 1, FILTER_BANDS)[None, :]
    feats = np.zeros((g.Lp, FILTER_EMB), np.float32)
    feats[:L] = np.concatenate([t, np.cos(f * w), -np.sin(f * w)], axis=-1)
    return fa, fbf, fbi, ct, st, feats


def _rope128(x, ct, st):
    return x * ct + pltpu.roll(x, QK_ROPE, 1) * st


def _proj_kernel(h_ref, g_ref, w1_ref, gq_ref, wuq_ref, gkv_ref, wukv_ref, ct_ref, st_ref,
                 q_ref, kn_ref, kr_ref, v_ref, u_ref):
    a = _rms(h_ref[...], g_ref[...]).astype(BF16)
    p = jnp.dot(a, w1_ref[...], preferred_element_type=F32)
    u_ref[...] = p[:, 512:].astype(BF16)
    ct = ct_ref[...]
    st = st_ref[...]
    kr_ref[...] = _rope128(p[:, 384:512], ct, st).astype(BF16)
    cq = _rms(p[:, :Q_LORA], gq_ref[...]).astype(BF16)
    q = jnp.dot(cq, wuq_ref[...], preferred_element_type=F32)
    for h in range(N_HEADS):
        lo = h * 2 * LANE
        q_ref[:, lo:lo + LANE] = (q[:, lo:lo + LANE] * Q_SCALE).astype(BF16)
        q_ref[:, lo + LANE:lo + 2 * LANE] = (_rope128(q[:, lo + LANE:lo + 2 * LANE], ct, st) * Q_SCALE).astype(BF16)
    ckv = _rms(p[:, Q_LORA:Q_LORA + KV_LORA], gkv_ref[...]).astype(BF16)
    kv = jnp.dot(ckv, wukv_ref[...], preferred_element_type=F32)
    kn_ref[...] = kv[:, :ATTN_WIDTH].astype(BF16)
    v_ref[...] = kv[:, ATTN_WIDTH:].astype(BF16)


def _proj(hp, g_pre, w1, gq, wuq, gkv, wukv, ct, st):
    B, Lp, _ = hp.shape
    tm = _pick_tile(Lp, 900, BF16_ROWS)
    row = lambda w: pl.BlockSpec((None, tm, w), lambda b, i: (b, i, 0))
    tab = pl.BlockSpec((tm, LANE), lambda b, i: (i, 0))
    outs = [jax.ShapeDtypeStruct((B, Lp, w), BF16) for w in (4 * 2 * LANE, ATTN_WIDTH, LANE, ATTN_WIDTH, 3 * HYENA_WIDTH)]
    return pl.pallas_call(
        _proj_kernel,
        grid=(B, Lp // tm),
        in_specs=[row(D_MODEL), _const_spec(g_pre.shape), _const_spec(w1.shape), _const_spec(gq.shape),
                  _const_spec(wuq.shape), _const_spec(gkv.shape), _const_spec(wukv.shape), tab, tab],
        out_specs=[row(4 * 2 * LANE), row(ATTN_WIDTH), row(LANE), row(ATTN_WIDTH), row(3 * HYENA_WIDTH)],
        out_shape=outs,
        compiler_params=pltpu.CompilerParams(dimension_semantics=("arbitrary", "arbitrary"), vmem_limit_bytes=VMEM_LIMIT),
        name="proj",
    )(hp, g_pre, w1, gq, wuq, gkv, wukv, ct, st)


def _attn_kernel(q_ref, kn_ref, kr_ref, v_ref, o_ref, *, L, Lp, tk):
    q = q_ref[...]
    tq = q.shape[0]

    def chunk(carry, start, size, masked):
        m, l, acc = carry
        k = jnp.concatenate([kn_ref[pl.ds(start, size), :], kr_ref[pl.ds(start, size), :]], axis=1)
        s = lax.dot_general(q, k, (((1,), (1,)), ((), ())), preferred_element_type=F32)
        if masked:
            col = lax.broadcasted_iota(jnp.int32, (1, size), 1)
            s = jnp.where(col < L - start, s, -1e30)
        m_new = jnp.maximum(m, jnp.max(s, axis=1, keepdims=True))
        alpha = jnp.exp2(m - m_new)
        p = jnp.exp2(s - m_new)
        l = alpha * l + jnp.sum(p, axis=1, keepdims=True)
        acc = alpha * acc + jnp.dot(p.astype(BF16), v_ref[pl.ds(start, size), :], preferred_element_type=F32)
        return m_new, l, acc

    n_full = (Lp - 1) // tk
    carry = (jnp.full((tq, 1), -1e30, F32), jnp.zeros((tq, 1), F32), jnp.zeros((tq, V_HEAD), F32))
    carry = lax.fori_loop(0, n_full, lambda i, c: chunk(c, pl.multiple_of(i * tk, tk), tk, False), carry)
    last = n_full * tk
    _, l, acc = chunk(carry, last, Lp - last, True)
    o_ref[...] = (acc / l).astype(BF16)


def _attention(q, kn, kr, v, L):
    B, Lp, _ = q.shape
    tq = _pick_tile(Lp, 640, BF16_ROWS)
    tk = 512
    full = lambda idx: pl.BlockSpec((None, Lp, LANE), idx)
    return pl.pallas_call(
        functools.partial(_attn_kernel, L=L, Lp=Lp, tk=tk),
        grid=(B, N_HEADS, Lp // tq),
        in_specs=[pl.BlockSpec((None, tq, 2 * LANE), lambda b, h, i: (b, i, h)),
                  full(lambda b, h, i: (b, 0, h)), full(lambda b, h, i: (b, 0, 0)), full(lambda b, h, i: (b, 0, h))],
        out_specs=pl.BlockSpec((None, tq, V_HEAD), lambda b, h, i: (b, i, h)),
        out_shape=jax.ShapeDtypeStruct((B, Lp, ATTN_WIDTH), BF16),
        compiler_params=pltpu.CompilerParams(dimension_semantics=("arbitrary",) * 3, vmem_limit_bytes=VMEM_LIMIT),
        name="attn",
    )(q, kn, kr, v)


def _filt_kernel(feat_ref, w1_ref, b1_ref, fr_ref, w2_ref, b2_ref, w3_ref, dec_ref, o_ref, *, L, tf):
    hp = lax.Precision.HIGHEST
    f = feat_ref[...]
    fr = fr_ref[...]
    h = jnp.sin(fr * (jnp.dot(f, w1_ref[...], precision=hp, preferred_element_type=F32) + b1_ref[...]))
    h = jnp.sin(fr * (jnp.dot(h, w2_ref[...], precision=hp, preferred_element_type=F32) + b2_ref[...]))
    h = jnp.dot(h, w3_ref[...], precision=hp, preferred_element_type=F32)
    h = h * jnp.exp(-f[:, 0:1] * jnp.abs(dec_ref[...]))
    row = pl.program_id(0) * tf + lax.broadcasted_iota(jnp.int32, (tf, 1), 0)
    col = lax.broadcasted_iota(jnp.int32, (1, h.shape[1]), 1)
    is_bwd = ((col // HYENA_WIDTH) % 2) == 1
    keep = (row < L) & jnp.logical_not(is_bwd & (row == 0))
    o_ref[...] = jnp.where(keep, h, 0.0)


def _filters(feats, f_w1, f_b1, f_freq, f_w2, f_b2, f_w3, f_decay, L):
    Lp = feats.shape[0]
    tf = _pick_tile(Lp, 640, SUBLANE)
    nf = f_w3.shape[1]
    args = (f_w1, f_b1, f_freq, f_w2, f_b2, f_w3, f_decay)
    return pl.pallas_call(
        functools.partial(_filt_kernel, L=L, tf=tf),
        grid=(Lp // tf,),
        in_specs=[pl.BlockSpec((tf, FILTER_EMB), lambda i: (i, 0))] + [_const_spec(a.shape) for a in args],
        out_specs=pl.BlockSpec((tf, nf), lambda i: (i, 0)),
        out_shape=jax.ShapeDtypeStruct((Lp, nf), F32),
        compiler_params=pltpu.CompilerParams(dimension_semantics=("arbitrary",), vmem_limit_bytes=VMEM_LIMIT),
        name="filt",
    )(feats, *args)


def _stage_a_fwd(zbuf, fa_ref, sbuf, g):
    def body(n2, c):
        x = zbuf[pl.ds(n2, g.n1hp, stride=LANE), :].astype(BF16)
        sbuf[pl.ds(pl.multiple_of(n2 * g.ka, SUBLANE), g.ka), :] = jnp.dot(fa_ref[n2], x, preferred_element_type=F32)
        return c

    lax.fori_loop(0, LANE, body, 0)


def _stage_a_inv(sbuf, fa_ref, ybuf, g):
    def body(n2, c):
        s = sbuf[pl.ds(pl.multiple_of(n2 * g.ka, SUBLANE), g.ka), :].astype(BF16)
        y = lax.dot_general(fa_ref[n2], s, (((0,), (0,)), ((), ())), preferred_element_type=F32)
        ybuf[pl.ds(n2, g.n1hp, stride=LANE), :] = y
        return c

    lax.fori_loop(0, LANE, body, 0)


def _load_k1(sbuf, k1, g):
    ar = sbuf[pl.ds(k1, LANE, stride=g.ka), :]
    ai = sbuf[pl.ds(g.k1p + k1, LANE, stride=g.ka), :]
    return jnp.concatenate([ar, ai], axis=0).astype(BF16)


def _spec_kernel(hf_ref, fa_ref, fbf_ref, out_ref, zbuf, sbuf, acc, *, g):
    d = pl.program_id(2)
    zbuf[: g.Lp, :] = hf_ref[...]
    if g.zrows > g.Lp:
        zbuf[g.Lp:, :] = jnp.zeros((g.zrows - g.Lp, LANE), F32)
    _stage_a_fwd(zbuf, fa_ref, sbuf, g)

    def run(backward):
        def body(k1, c):
            x = jnp.dot(fbf_ref[...], _load_k1(sbuf, k1, g), preferred_element_type=F32)
            scale = jnp.where((k1 == 0) | (k1 == g.n1 // 2), 1.0, 2.0) / g.N
            xr = x[:LANE] * scale
            xi = x[LANE:] * scale
            if backward:
                out_ref[k1, 0] = (acc[k1, 0] + xr).astype(BF16)
                out_ref[k1, 1] = (acc[k1, 1] - xi).astype(BF16)
            else:
                acc[k1, 0] = xr
                acc[k1, 1] = xi
            return c

        lax.fori_loop(0, g.k1, body, 0)

    pl.when(d == 0)(lambda: run(False))
    pl.when(d == 1)(lambda: run(True))


def _spectrum(hf, fa, fbf, g):
    ncb = HYENA_WIDTH // LANE
    return pl.pallas_call(
        functools.partial(_spec_kernel, g=g),
        grid=(HYENA_ORDER, ncb, 2),
        in_specs=[pl.BlockSpec((g.Lp, LANE), lambda o, c, d: (0, (o * 2 + d) * ncb + c)),
                  _const_spec(fa.shape), _const_spec(fbf.shape)],
        out_specs=pl.BlockSpec((None, g.k1, 2, LANE, LANE), lambda o, c, d: (o, 0, 0, 0, c)),
        out_shape=jax.ShapeDtypeStruct((HYENA_ORDER, g.k1, 2, LANE, HYENA_WIDTH), BF16),
        scratch_shapes=[pltpu.VMEM((g.zrows, LANE), F32), pltpu.VMEM((LANE * g.ka, LANE), F32),
                        pltpu.VMEM((g.k1, 2, LANE, LANE), F32)],
        compiler_params=pltpu.CompilerParams(dimension_semantics=("arbitrary",) * 3, vmem_limit_bytes=VMEM_LIMIT),
        name="spec",
    )(hf, fa, fbf)


def _short_conv(u_ref, w_ref, r0, t, Lp):
    halo = BF16_ROWS
    parts = []
    if r0 == 0:
        parts.append(jnp.zeros((halo, LANE), F32))
    lo = max(r0 - halo, 0)
    hi = min(r0 + t + halo, Lp)
    parts.append(u_ref[lo:hi, :].astype(F32))
    if r0 + t == Lp:
        parts.append(jnp.zeros((halo, LANE), F32))
    x = jnp.concatenate(parts, axis=0) if len(parts) > 1 else parts[0]
    n = t + 2 * halo
    xm1 = pltpu.roll(x, 1, 0)[halo:halo + t]
    xp1 = pltpu.roll(x, n - 1, 0)[halo:halo + t]
    return xm1 * w_ref[0:1, :] + x[halo:halo + t] * w_ref[1:2, :] + xp1 * w_ref[2:3, :]


def _hyena_kernel(gate_ref, v_ref, wg_ref, wv_ref, bias_ref, h_ref, fa_ref, fbf_ref, fbi_ref,
                  out_ref, zbuf, ybuf, sbuf, *, L, g, t):
    order = pl.program_id(2)
    Lp = g.Lp
    nchunk = Lp // t

    def row_mask(r0):
        return (r0 + lax.broadcasted_iota(jnp.int32, (t, 1), 0)) < L

    @pl.when(order == 0)
    def _():
        if g.zrows > Lp:
            zbuf[Lp:, :] = jnp.zeros((g.zrows - Lp, LANE), F32)
        for c in range(nchunk):
            r0 = c * t
            zbuf[r0:r0 + t, :] = jnp.where(row_mask(r0), _short_conv(v_ref, wv_ref, r0, t, Lp), 0.0)

    _stage_a_fwd(zbuf, fa_ref, sbuf, g)

    def body(k1, c):
        x = jnp.dot(fbf_ref[...], _load_k1(sbuf, k1, g), preferred_element_type=F32)
        xr, xi = x[:LANE], x[LANE:]
        hr = h_ref[k1, 0].astype(F32)
        hi = h_ref[k1, 1].astype(F32)
        y = jnp.concatenate([xr * hr - xi * hi, xr * hi + xi * hr], axis=0).astype(BF16)
        b = jnp.dot(fbi_ref[...], y, preferred_element_type=F32)
        sbuf[pl.ds(k1, LANE, stride=g.ka), :] = b[:LANE]
        sbuf[pl.ds(g.k1p + k1, LANE, stride=g.ka), :] = b[LANE:]
        return c

    lax.fori_loop(0, g.k1, body, 0)
    _stage_a_inv(sbuf, fa_ref, ybuf, g)

    bias = bias_ref[...]
    for c in range(nchunk):
        r0 = c * t
        gate = _short_conv(gate_ref, wg_ref, r0, t, Lp)
        z = zbuf[r0:r0 + t, :]
        zn = jnp.where(row_mask(r0), gate * (ybuf[r0:r0 + t, :] + bias * z), 0.0)
        zbuf[r0:r0 + t, :] = zn
        out_ref[r0:r0 + t, :] = zn.astype(BF16)


def _hyena(u, conv_w, hy_bias, hspec, fa, fbf, fbi, L, g):
    B, Lp, _ = u.shape
    ncb = HYENA_WIDTH // LANE
    t = _pick_tile(Lp, 900, BF16_ROWS)
    col = lambda idx: pl.BlockSpec((None, Lp, LANE), idx)
    wcol = lambda idx: pl.BlockSpec((3, LANE), idx)
    return pl.pallas_call(
        functools.partial(_hyena_kernel, L=L, g=g, t=t),
        grid=(ncb, B, HYENA_ORDER),
        in_specs=[col(lambda c, b, o: (b, 0, o * ncb + c)), col(lambda c, b, o: (b, 0, HYENA_ORDER * ncb + c)),
                  wcol(lambda c, b, o: (0, o * ncb + c)), wcol(lambda c, b, o: (0, HYENA_ORDER * ncb + c)),
                  pl.BlockSpec((None, 1, LANE), lambda c, b, o: (o, 0, c)),
                  pl.BlockSpec((None, g.k1, 2, LANE, LANE), lambda c, b, o: (o, 0, 0, 0, c)),
                  _const_spec(fa.shape), _const_spec(fbf.shape), _const_spec(fbi.shape)],
        out_specs=col(lambda c, b, o: (b, 0, c)),
        out_shape=jax.ShapeDtypeStruct((B, Lp, HYENA_WIDTH), BF16),
        scratch_shapes=[pltpu.VMEM((g.zrows, LANE), F32), pltpu.VMEM((g.zrows, LANE), F32),
                        pltpu.VMEM((LANE * g.ka, LANE), F32)],
        compiler_params=pltpu.CompilerParams(dimension_semantics=("arbitrary",) * 3, vmem_limit_bytes=VMEM_LIMIT),
        name="hyena",
    )(u, u, conv_w, conv_w, hy_bias, hspec, fa, fbf, fbi)


def _mlp_kernel(h_ref, oa_ref, oh_ref, ga_ref, gh_ref, wo_ref, gpost_ref, gpre_ref, w1_ref, w2_ref, gmlp_ref, o_ref):
    oa = _rms(oa_ref[...].astype(F32), ga_ref[...]).astype(BF16)
    oh = _rms(oh_ref[...].astype(F32), gh_ref[...]).astype(BF16)
    mix = jnp.dot(jnp.concatenate([oa, oh], axis=1), wo_ref[...], preferred_element_type=F32)
    h1 = h_ref[...] + _rms(mix, gpost_ref[...])
    m = jnp.dot(_rms(h1, gpre_ref[...]).astype(BF16), w1_ref[...], preferred_element_type=F32)
    m = jnp.square(jnp.maximum(m, 0.0)).astype(BF16)
    m = jnp.dot(m, w2_ref[...], preferred_element_type=F32)
    o_ref[...] = h1 + _rms(m, gmlp_ref[...])


def _mlp(hp, oa, oh, ga, gh, wo, gpost, gpre, w1, w2, gmlp):
    R = hp.shape[0]
    tm = _pick_tile(R, 512, BF16_ROWS)
    row = lambda w: pl.BlockSpec((tm, w), lambda i: (i, 0))
    consts = (ga, gh, wo, gpost, gpre, w1, w2, gmlp)
    return pl.pallas_call(
        _mlp_kernel,
        grid=(R // tm,),
        in_specs=[row(D_MODEL), row(ATTN_WIDTH), row(HYENA_WIDTH)] + [_const_spec(a.shape) for a in consts],
        out_specs=row(D_MODEL),
        out_shape=jax.ShapeDtypeStruct((R, D_MODEL), F32),
        compiler_params=pltpu.CompilerParams(dimension_semantics=("arbitrary",), vmem_limit_bytes=VMEM_LIMIT),
        name="mlp",
    )(hp, oa, oh, *consts)


def _rot_cols(w):
    half = QK_ROPE // 2
    return jnp.concatenate([-w[:, half:], w[:, :half]], axis=1)


def _prep_weights(w_in, w_uq, w_ukv):
    s1, s2, s3 = Q_LORA, Q_LORA + KV_LORA, Q_LORA + KV_LORA + QK_ROPE
    kr = w_in[:, s2:s3]
    w1 = jnp.concatenate([w_in[:, :s2], kr, _rot_cols(kr), w_in[:, s3:]], axis=1).astype(BF16)
    qh = QK_NOPE + QK_ROPE
    cols = []
    for h in range(N_HEADS):
        rope = w_uq[:, h * qh + QK_NOPE:(h + 1) * qh]
        cols += [w_uq[:, h * qh:h * qh + QK_NOPE], rope, _rot_cols(rope)]
    wuq = jnp.concatenate(cols, axis=1).astype(BF16)
    kvh = QK_NOPE + V_HEAD
    kcols = [w_ukv[:, h * kvh:h * kvh + QK_NOPE] for h in range(N_HEADS)]
    vcols = [w_ukv[:, h * kvh + QK_NOPE:(h + 1) * kvh] for h in range(N_HEADS)]
    wukv = jnp.concatenate(kcols + vcols, axis=1).astype(BF16)
    return w1, wuq, wukv


def _trunk(x, meta_tokens, pre_mix_g, w1, q_norm_g, wuq, kv_norm_g, wukv, conv_w,
           f_w1, f_b1, f_freq, f_w2, f_b2, f_w3, f_decay, hy_bias, attn_out_g, hy_out_g,
           wo, post_mix_g, pre_mlp_g, wff1, wff2, post_mlp_g):
    B, S, _ = x.shape
    L = S + N_META
    g = _Geom(L)
    fa, fbf, fbi, ct, st, feats = _tables(L)
    fa, fbf, fbi = (jnp.asarray(a).astype(BF16) for a in (fa, fbf, fbi))
    meta = jnp.broadcast_to(meta_tokens.astype(x.dtype)[None], (B, N_META, D_MODEL))
    hp = jnp.concatenate([meta, x, jnp.zeros((B, g.Lp - L, D_MODEL), x.dtype)], axis=1)

    q, kn, kr, v, u = _proj(hp, pre_mix_g, w1, q_norm_g, wuq, kv_norm_g, wukv, jnp.asarray(ct), jnp.asarray(st))
    o_attn = _attention(q, kn, kr, v, L)
    hf = _filters(jnp.asarray(feats), f_w1, f_b1, f_freq, f_w2, f_b2, f_w3, f_decay, L)
    hspec = _spectrum(hf, fa, fbf, g)
    o_hy = _hyena(u, conv_w, hy_bias, hspec, fa, fbf, fbi, L, g)
    R = B * g.Lp
    out = _mlp(hp.reshape(R, D_MODEL), o_attn.reshape(R, ATTN_WIDTH), o_hy.reshape(R, HYENA_WIDTH),
               attn_out_g, hy_out_g, wo, post_mix_g, pre_mlp_g, wff1, wff2, post_mlp_g)
    return out.reshape(B, g.Lp, D_MODEL)[:, N_META:L]


def kernel(x_prompt, x_sample, meta_tokens, pre_mix_g, w_in, q_norm_g, w_uq, kv_norm_g, w_ukv, conv_w, f_w1, f_b1, f_freq, f_w2, f_b2, f_w3, f_decay, hy_bias, attn_out_g, hy_out_g, w_o, post_mix_g, pre_mlp_g, w_ff1, w_ff2, post_mlp_g):
    assert pre_mix_g.shape[0] == 1, "single-layer trunk"
    w1, wuq, wukv = _prep_weights(w_in[0], w_uq[0], w_ukv[0])
    shared = (meta_tokens, pre_mix_g, w1, q_norm_g, wuq, kv_norm_g, wukv, conv_w[0],
              f_w1[0], f_b1, f_freq, f_w2[0], f_b2, f_w3[0], f_decay, hy_bias[0][:, None, :], attn_out_g, hy_out_g,
              w_o[0].astype(BF16), post_mix_g, pre_mlp_g, w_ff1[0].astype(BF16), w_ff2[0].astype(BF16), post_mlp_g)
    return (_trunk(x_prompt, *shared), _trunk(x_sample, *shared))
```

```python
import functools
import math

import numpy as np
import jax
import jax.numpy as jnp
from jax import lax
from jax.experimental import pallas as pl
from jax.experimental.pallas import tpu as pltpu

F32 = jnp.float32
BF16 = jnp.bfloat16

D_MODEL = 1024
N_META = 16
ATTN_WIDTH = 512
HYENA_WIDTH = 512
N_HEADS = 4
QK_NOPE = 128
QK_ROPE = 64
V_HEAD = 128
Q_LORA = 256
KV_LORA = 128
ROPE_BASE = 10000.0
FILTER_EMB = 33
FILTER_BANDS = 16
HYENA_ORDER = 2
D_FF = 4096
NORM_EPS = 1e-6

LANE = 128
SUBLANE = 8
BF16_ROWS = 16
VMEM_LIMIT = 56 * 1024 * 1024
HYENA_VMEM_BUDGET = 46 * 1024 * 1024
STAGE_A_UNROLL = 16
STAGE_B_UNROLL = 4
ATTN_KV_UNROLL = 2

Q_SCALE = (QK_NOPE + QK_ROPE) ** -0.5 * math.log2(math.e)


def _round_up(x, m):
    return -(-x // m) * m


def _pick_tile(n, cap, mult):
    best = None
    for d in range(mult, min(n, cap) + 1, mult):
        if n % d == 0:
            best = d
    return n if best is None else best


def _rms(x, g):
    return x * lax.rsqrt(jnp.mean(x * x, axis=-1, keepdims=True) + NORM_EPS) * g


def _const_spec(shape):
    nd = len(shape)
    return pl.BlockSpec(shape, lambda *_: (0,) * nd, pipeline_mode=pl.Buffered(1))


class _Geom:
    def __init__(self, L):
        self.L = L
        self.n1h = -(-L // LANE)
        n1 = -(-(2 * L - 1) // LANE)
        self.n1 = n1 + (n1 % 2)
        self.N = self.n1 * LANE
        self.k1 = self.n1 // 2 + 1
        self.k1p = _round_up(self.k1, SUBLANE)
        self.ka = 2 * self.k1p
        self.n1hp = _round_up(self.n1h, BF16_ROWS)
        self.Lp = self.n1h * LANE
        self.zrows = self.n1hp * LANE


@functools.lru_cache(maxsize=None)
def _tables(L):
    g = _Geom(L)
    n2 = np.arange(LANE)[:, None, None]
    k1 = np.arange(g.k1)[None, :, None]
    n1 = np.arange(g.n1hp)[None, None, :]
    n = n1 * LANE + n2
    ang = 2.0 * np.pi * ((k1 * n) % g.N).astype(np.float64) / g.N
    valid = (n1 < g.n1h)
    fa = np.zeros((LANE, g.ka, g.n1hp), np.float32)
    fa[:, : g.k1, :] = np.cos(ang) * valid
    fa[:, g.k1p : g.k1p + g.k1, :] = -np.sin(ang) * valid
    kk = np.arange(LANE)
    th = 2.0 * np.pi * ((kk[:, None] * kk[None, :]) % LANE) / LANE
    c, s = np.cos(th), np.sin(th)
    fbf = np.block([[c, s], [-s, c]]).astype(np.float32)
    fbi = np.block([[c, -s], [s, c]]).astype(np.float32)
    inv = 1.0 / (ROPE_BASE ** (np.arange(0, QK_ROPE, 2, dtype=np.float64) / QK_ROPE))
    pos = np.arange(g.Lp, dtype=np.float64)[:, None] * inv[None, :]
    pos = np.concatenate([pos, pos], axis=-1)
    ct = np.zeros((g.Lp, LANE), np.float32)
    st = np.zeros((g.Lp, LANE), np.float32)
    ct[:, :QK_ROPE] = np.cos(pos)
    st[:, :QK_ROPE] = np.sin(pos)
    t = np.linspace(0.0, 1.0, L)[:, None]
    w = 2.0 * np.pi * np.arange(L, dtype=np.float64)[:, None] / L
    f = np.linspace(1e-4, FILTER_BANDS - 1, FILTER_BANDS)[None, :]
    feats = np.zeros((g.Lp, FILTER_EMB), np.float32)
    feats[:L] = np.concatenate([t, np.cos(f * w), -np.sin(f * w)], axis=-1)
    return fa, fbf, fbi, ct, st, feats


def _rope128(x, ct, st):
    return x * ct + pltpu.roll(x, QK_ROPE, 1) * st


def _proj_kernel(h_ref, g_ref, w1_ref, gq_ref, wuq_ref, gkv_ref, wukv_ref, ct_ref, st_ref,
                 q_ref, kn_ref, kr_ref, v_ref, u_ref):
    a = _rms(h_ref[...], g_ref[...]).astype(BF16)
    p = jnp.dot(a, w1_ref[...], preferred_element_type=F32)
    u_ref[...] = p[:, 512:].astype(BF16)
    ct = ct_ref[...]
    st = st_ref[...]
    kr_ref[...] = _rope128(p[:, 384:512], ct, st).astype(BF16)
    cq = _rms(p[:, :Q_LORA], gq_ref[...]).astype(BF16)
    q = jnp.dot(cq, wuq_ref[...], preferred_element_type=F32)
    for h in range(N_HEADS):
        lo = h * 2 * LANE
        q_ref[:, lo:lo + LANE] = (q[:, lo:lo + LANE] * Q_SCALE).astype(BF16)
        q_ref[:, lo + LANE:lo + 2 * LANE] = (_rope128(q[:, lo + LANE:lo + 2 * LANE], ct, st) * Q_SCALE).astype(BF16)
    ckv = _rms(p[:, Q_LORA:Q_LORA + KV_LORA], gkv_ref[...]).astype(BF16)
    kv = jnp.dot(ckv, wukv_ref[...], preferred_element_type=F32)
    kn_ref[...] = kv[:, :ATTN_WIDTH].astype(BF16)
    v_ref[...] = kv[:, ATTN_WIDTH:].astype(BF16)


def _proj(hp, g_pre, w1, gq, wuq, gkv, wukv, ct, st):
    B, Lp, _ = hp.shape
    tm = _pick_tile(Lp, 900, BF16_ROWS)
    row = lambda w: pl.BlockSpec((None, tm, w), lambda b, i: (b, i, 0))
    tab = pl.BlockSpec((tm, LANE), lambda b, i: (i, 0))
    outs = [jax.ShapeDtypeStruct((B, Lp, w), BF16) for w in (4 * 2 * LANE, ATTN_WIDTH, LANE, ATTN_WIDTH, 3 * HYENA_WIDTH)]
    return pl.pallas_call(
        _proj_kernel,
        grid=(B, Lp // tm),
        in_specs=[row(D_MODEL), _const_spec(g_pre.shape), _const_spec(w1.shape), _const_spec(gq.shape),
                  _const_spec(wuq.shape), _const_spec(gkv.shape), _const_spec(wukv.shape), tab, tab],
        out_specs=[row(4 * 2 * LANE), row(ATTN_WIDTH), row(LANE), row(ATTN_WIDTH), row(3 * HYENA_WIDTH)],
        out_shape=outs,
        compiler_params=pltpu.CompilerParams(dimension_semantics=("arbitrary", "arbitrary"), vmem_limit_bytes=VMEM_LIMIT),
        name="proj",
    )(hp, g_pre, w1, gq, wuq, gkv, wukv, ct, st)


def _attn_kernel(q_ref, kn_ref, kr_ref, v_ref, o_ref, *, L, Lp, tk):
    q = q_ref[...]
    tq = q.shape[0]

    def chunk(carry, start, size, masked):
        m, l, acc = carry
        k = jnp.concatenate([kn_ref[pl.ds(start, size), :], kr_ref[pl.ds(start, size), :]], axis=1)
        s = lax.dot_general(q, k, (((1,), (1,)), ((), ())), preferred_element_type=F32)
        if masked:
            col = lax.broadcasted_iota(jnp.int32, (1, size), 1)
            s = jnp.where(col < L - start, s, -1e30)
        m_new = jnp.maximum(m, jnp.max(s, axis=1, keepdims=True))
        alpha = jnp.exp2(m - m_new)
        p = jnp.exp2(s - m_new)
        l = alpha * l + jnp.sum(p, axis=1, keepdims=True)
        acc = alpha * acc + jnp.dot(p.astype(BF16), v_ref[pl.ds(start, size), :], preferred_element_type=F32)
        return m_new, l, acc

    n_full = (Lp - 1) // tk
    carry = (jnp.full((tq, 1), -1e30, F32), jnp.zeros((tq, 1), F32), jnp.zeros((tq, V_HEAD), F32))
    carry = lax.fori_loop(0, n_full, lambda i, c: chunk(c, pl.multiple_of(i * tk, tk), tk, False), carry,
                          unroll=ATTN_KV_UNROLL)
    last = n_full * tk
    _, l, acc = chunk(carry, last, Lp - last, True)
    o_ref[...] = (acc / l).astype(BF16)


def _attention(q, kn, kr, v, L):
    B, Lp, _ = q.shape
    tq = _pick_tile(Lp, 640, BF16_ROWS)
    tk = 512
    full = lambda idx: pl.BlockSpec((None, Lp, LANE), idx)
    return pl.pallas_call(
        functools.partial(_attn_kernel, L=L, Lp=Lp, tk=tk),
        grid=(B, N_HEADS, Lp // tq),
        in_specs=[pl.BlockSpec((None, tq, 2 * LANE), lambda b, h, i: (b, i, h)),
                  full(lambda b, h, i: (b, 0, h)), full(lambda b, h, i: (b, 0, 0)), full(lambda b, h, i: (b, 0, h))],
        out_specs=pl.BlockSpec((None, tq, V_HEAD), lambda b, h, i: (b, i, h)),
        out_shape=jax.ShapeDtypeStruct((B, Lp, ATTN_WIDTH), BF16),
        compiler_params=pltpu.CompilerParams(dimension_semantics=("arbitrary",) * 3, vmem_limit_bytes=VMEM_LIMIT),
        name="attn",
    )(q, kn, kr, v)


def _filt_kernel(feat_ref, w1_ref, b1_ref, fr_ref, w2_ref, b2_ref, w3_ref, dec_ref, o_ref, *, L, tf):
    hp = lax.Precision.HIGHEST
    f = feat_ref[...]
    fr = fr_ref[...]
    h = jnp.sin(fr * (jnp.dot(f, w1_ref[...], precision=hp, preferred_element_type=F32) + b1_ref[...]))
    h = jnp.sin(fr * (jnp.dot(h, w2_ref[...], precision=hp, preferred_element_type=F32) + b2_ref[...]))
    h = jnp.dot(h, w3_ref[...], precision=hp, preferred_element_type=F32)
    h = h * jnp.exp(-f[:, 0:1] * jnp.abs(dec_ref[...]))
    row = pl.program_id(0) * tf + lax.broadcasted_iota(jnp.int32, (tf, 1), 0)
    col = lax.broadcasted_iota(jnp.int32, (1, h.shape[1]), 1)
    is_bwd = ((col // HYENA_WIDTH) % 2) == 1
    keep = (row < L) & jnp.logical_not(is_bwd & (row == 0))
    o_ref[...] = jnp.where(keep, h, 0.0)


def _filters(feats, f_w1, f_b1, f_freq, f_w2, f_b2, f_w3, f_decay, L):
    Lp = feats.shape[0]
    tf = _pick_tile(Lp, 640, SUBLANE)
    nf = f_w3.shape[1]
    args = (f_w1, f_b1, f_freq, f_w2, f_b2, f_w3, f_decay)
    return pl.pallas_call(
        functools.partial(_filt_kernel, L=L, tf=tf),
        grid=(Lp // tf,),
        in_specs=[pl.BlockSpec((tf, FILTER_EMB), lambda i: (i, 0))] + [_const_spec(a.shape) for a in args],
        out_specs=pl.BlockSpec((tf, nf), lambda i: (i, 0)),
        out_shape=jax.ShapeDtypeStruct((Lp, nf), F32),
        compiler_params=pltpu.CompilerParams(dimension_semantics=("arbitrary",), vmem_limit_bytes=VMEM_LIMIT),
        name="filt",
    )(feats, *args)


def _load_planes(buf, rows):
    return jnp.concatenate([buf[j, rows, :] for j in range(buf.shape[0])], axis=1)


def _store_planes(buf, rows, val):
    for j in range(buf.shape[0]):
        buf[j, rows, :] = val[:, j * LANE:(j + 1) * LANE]


def _stage_a_fwd(zbuf, fa_ref, sbuf, g):
    def body(n2, c):
        x = _load_planes(zbuf, pl.ds(n2, g.n1hp, stride=LANE)).astype(BF16)
        _store_planes(sbuf, pl.ds(pl.multiple_of(n2 * g.ka, SUBLANE), g.ka),
                      jnp.dot(fa_ref[n2], x, preferred_element_type=F32))
        return c

    lax.fori_loop(0, LANE, body, 0, unroll=STAGE_A_UNROLL)


def _stage_a_inv(sbuf, fa_ref, ybuf, g):
    def body(n2, c):
        s = _load_planes(sbuf, pl.ds(pl.multiple_of(n2 * g.ka, SUBLANE), g.ka)).astype(BF16)
        y = lax.dot_general(fa_ref[n2], s, (((0,), (0,)), ((), ())), preferred_element_type=F32)
        _store_planes(ybuf, pl.ds(n2, g.n1hp, stride=LANE), y)
        return c

    lax.fori_loop(0, LANE, body, 0, unroll=STAGE_A_UNROLL)


def _load_k1(sbuf, k1, g):
    ar = _load_planes(sbuf, pl.ds(k1, LANE, stride=g.ka))
    ai = _load_planes(sbuf, pl.ds(g.k1p + k1, LANE, stride=g.ka))
    return jnp.concatenate([ar, ai], axis=0).astype(BF16)


def _plane_scratch(rows, cb):
    return pltpu.VMEM((cb // LANE, rows, LANE), F32)


def _hyena_cb(g, bytes_per_lane):
    fixed = LANE * g.ka * LANE * 2
    for cb in (HYENA_WIDTH, HYENA_WIDTH // 2, LANE):
        if fixed + cb * bytes_per_lane <= HYENA_VMEM_BUDGET:
            return cb
    return LANE


def _spec_kernel(hf_ref, fa_ref, fbf_ref, out_ref, zbuf, sbuf, acc, *, g):
    d = pl.program_id(2)
    _store_planes(zbuf, slice(0, g.Lp), hf_ref[...])
    if g.zrows > g.Lp:
        zbuf[:, g.Lp:, :] = jnp.zeros((zbuf.shape[0], g.zrows - g.Lp, LANE), F32)
    _stage_a_fwd(zbuf, fa_ref, sbuf, g)

    def run(backward):
        def body(k1, c):
            x = jnp.dot(fbf_ref[...], _load_k1(sbuf, k1, g), preferred_element_type=F32)
            scale = jnp.where((k1 == 0) | (k1 == g.n1 // 2), 1.0, 2.0) / g.N
            xr = x[:LANE] * scale
            xi = x[LANE:] * scale
            if backward:
                out_ref[k1, 0] = (acc[k1, 0] + xr).astype(BF16)
                out_ref[k1, 1] = (acc[k1, 1] - xi).astype(BF16)
            else:
                acc[k1, 0] = xr
                acc[k1, 1] = xi
            return c

        lax.fori_loop(0, g.k1, body, 0, unroll=STAGE_B_UNROLL)

    pl.when(d == 0)(lambda: run(False))
    pl.when(d == 1)(lambda: run(True))


def _spectrum(hf, fa, fbf, g):
    cb = _hyena_cb(g, 4 * (g.zrows + LANE * g.ka + g.k1 * 2 * LANE) + 2 * 4 * g.Lp + 2 * 2 * g.k1 * 2 * LANE)
    ncb = HYENA_WIDTH // cb
    return pl.pallas_call(
        functools.partial(_spec_kernel, g=g),
        grid=(HYENA_ORDER, ncb, 2),
        in_specs=[pl.BlockSpec((g.Lp, cb), lambda o, c, d: (0, (o * 2 + d) * ncb + c)),
                  _const_spec(fa.shape), _const_spec(fbf.shape)],
        out_specs=pl.BlockSpec((None, g.k1, 2, LANE, cb), lambda o, c, d: (o, 0, 0, 0, c)),
        out_shape=jax.ShapeDtypeStruct((HYENA_ORDER, g.k1, 2, LANE, HYENA_WIDTH), BF16),
        scratch_shapes=[_plane_scratch(g.zrows, cb), _plane_scratch(LANE * g.ka, cb),
                        pltpu.VMEM((g.k1, 2, LANE, cb), F32)],
        compiler_params=pltpu.CompilerParams(dimension_semantics=("arbitrary",) * 3, vmem_limit_bytes=VMEM_LIMIT),
        name="spec",
    )(hf, fa, fbf)


def _short_conv(u_ref, w_ref, r0, t, Lp):
    halo = BF16_ROWS
    cb = u_ref.shape[1]
    parts = []
    if r0 == 0:
        parts.append(jnp.zeros((halo, cb), F32))
    lo = max(r0 - halo, 0)
    hi = min(r0 + t + halo, Lp)
    parts.append(u_ref[lo:hi, :].astype(F32))
    if r0 + t == Lp:
        parts.append(jnp.zeros((halo, cb), F32))
    x = jnp.concatenate(parts, axis=0) if len(parts) > 1 else parts[0]
    n = t + 2 * halo
    xm1 = pltpu.roll(x, 1, 0)[halo:halo + t]
    xp1 = pltpu.roll(x, n - 1, 0)[halo:halo + t]
    return xm1 * w_ref[0:1, :] + x[halo:halo + t] * w_ref[1:2, :] + xp1 * w_ref[2:3, :]


def _hyena_kernel(gate_ref, v_ref, wg_ref, wv_ref, bias_ref, h_ref, fa_ref, fbf_ref, fbi_ref,
                  out_ref, zbuf, ybuf, sbuf, *, L, g, t):
    order = pl.program_id(2)
    Lp = g.Lp
    nchunk = Lp // t

    def row_mask(r0):
        return (r0 + lax.broadcasted_iota(jnp.int32, (t, 1), 0)) < L

    @pl.when(order == 0)
    def _():
        if g.zrows > Lp:
            zbuf[:, Lp:, :] = jnp.zeros((zbuf.shape[0], g.zrows - Lp, LANE), F32)
        for c in range(nchunk):
            r0 = c * t
            _store_planes(zbuf, slice(r0, r0 + t),
                          jnp.where(row_mask(r0), _short_conv(v_ref, wv_ref, r0, t, Lp), 0.0))

    _stage_a_fwd(zbuf, fa_ref, sbuf, g)

    def body(k1, c):
        x = jnp.dot(fbf_ref[...], _load_k1(sbuf, k1, g), preferred_element_type=F32)
        xr, xi = x[:LANE], x[LANE:]
        hr = h_ref[k1, 0].astype(F32)
        hi = h_ref[k1, 1].astype(F32)
        y = jnp.concatenate([xr * hr - xi * hi, xr * hi + xi * hr], axis=0).astype(BF16)
        b = jnp.dot(fbi_ref[...], y, preferred_element_type=F32)
        _store_planes(sbuf, pl.ds(k1, LANE, stride=g.ka), b[:LANE])
        _store_planes(sbuf, pl.ds(g.k1p + k1, LANE, stride=g.ka), b[LANE:])
        return c

    lax.fori_loop(0, g.k1, body, 0, unroll=STAGE_B_UNROLL)
    _stage_a_inv(sbuf, fa_ref, ybuf, g)

    bias = bias_ref[...]
    for c in range(nchunk):
        r0 = c * t
        gate = _short_conv(gate_ref, wg_ref, r0, t, Lp)
        rows = slice(r0, r0 + t)
        zn = jnp.where(row_mask(r0), gate * (_load_planes(ybuf, rows) + bias * _load_planes(zbuf, rows)), 0.0)
        _store_planes(zbuf, rows, zn)
        out_ref[rows, :] = zn.astype(BF16)


def _hyena(u, conv_w, hy_bias, hspec, fa, fbf, fbi, L, g):
    B, Lp, _ = u.shape
    cb = _hyena_cb(g, 4 * (2 * g.zrows + LANE * g.ka) + 2 * 2 * g.k1 * 2 * LANE + 6 * 2 * Lp)
    ncb = HYENA_WIDTH // cb
    t = _pick_tile(Lp, 900 * LANE // cb, BF16_ROWS)
    col = lambda idx: pl.BlockSpec((None, Lp, cb), idx)
    wcol = lambda idx: pl.BlockSpec((3, cb), idx)
    return pl.pallas_call(
        functools.partial(_hyena_kernel, L=L, g=g, t=t),
        grid=(ncb, B, HYENA_ORDER),
        in_specs=[col(lambda c, b, o: (b, 0, o * ncb + c)), col(lambda c, b, o: (b, 0, HYENA_ORDER * ncb + c)),
                  wcol(lambda c, b, o: (0, o * ncb + c)), wcol(lambda c, b, o: (0, HYENA_ORDER * ncb + c)),
                  pl.BlockSpec((None, 1, cb), lambda c, b, o: (o, 0, c)),
                  pl.BlockSpec((None, g.k1, 2, LANE, cb), lambda c, b, o: (o, 0, 0, 0, c)),
                  _const_spec(fa.shape), _const_spec(fbf.shape), _const_spec(fbi.shape)],
        out_specs=col(lambda c, b, o: (b, 0, c)),
        out_shape=jax.ShapeDtypeStruct((B, Lp, HYENA_WIDTH), BF16),
        scratch_shapes=[_plane_scratch(g.zrows, cb), _plane_scratch(g.zrows, cb), _plane_scratch(LANE * g.ka, cb)],
        compiler_params=pltpu.CompilerParams(dimension_semantics=("arbitrary",) * 3, vmem_limit_bytes=VMEM_LIMIT),
        name="hyena",
    )(u, u, conv_w, conv_w, hy_bias, hspec, fa, fbf, fbi)


def _mlp_kernel(h_ref, oa_ref, oh_ref, ga_ref, gh_ref, wo_ref, gpost_ref, gpre_ref, w1_ref, w2_ref, gmlp_ref, o_ref):
    oa = _rms(oa_ref[...].astype(F32), ga_ref[...]).astype(BF16)
    oh = _rms(oh_ref[...].astype(F32), gh_ref[...]).astype(BF16)
    mix = jnp.dot(jnp.concatenate([oa, oh], axis=1), wo_ref[...], preferred_element_type=F32)
    h1 = h_ref[...] + _rms(mix, gpost_ref[...])
    m = jnp.dot(_rms(h1, gpre_ref[...]).astype(BF16), w1_ref[...], preferred_element_type=F32)
    m = jnp.square(jnp.maximum(m, 0.0)).astype(BF16)
    m = jnp.dot(m, w2_ref[...], preferred_element_type=F32)
    o_ref[...] = h1 + _rms(m, gmlp_ref[...])


def _mlp(x, oa, oh, ga, gh, wo, gpost, gpre, w1, w2, gmlp):
    B, S, _ = x.shape
    Lp = oa.shape[0] // B
    tm = _pick_tile(S, 512, BF16_ROWS)
    row = pl.BlockSpec((None, tm, D_MODEL), lambda b, i: (b, i, 0))
    shifted = lambda w: pl.BlockSpec((pl.Element(tm), pl.Element(w)),
                                     lambda b, i: (pl.multiple_of(b * Lp + N_META + i * tm, BF16_ROWS), 0))
    consts = (ga, gh, wo, gpost, gpre, w1, w2, gmlp)
    return pl.pallas_call(
        _mlp_kernel,
        grid=(B, S // tm),
        in_specs=[row, shifted(ATTN_WIDTH), shifted(HYENA_WIDTH)] + [_const_spec(a.shape) for a in consts],
        out_specs=row,
        out_shape=jax.ShapeDtypeStruct((B, S, D_MODEL), F32),
        compiler_params=pltpu.CompilerParams(dimension_semantics=("arbitrary",) * 2, vmem_limit_bytes=VMEM_LIMIT),
        name="mlp",
    )(x, oa, oh, *consts)


def _rot_cols(w):
    half = QK_ROPE // 2
    return jnp.concatenate([-w[:, half:], w[:, :half]], axis=1)


def _prep_weights(w_in, w_uq, w_ukv):
    s1, s2, s3 = Q_LORA, Q_LORA + KV_LORA, Q_LORA + KV_LORA + QK_ROPE
    kr = w_in[:, s2:s3]
    w1 = jnp.concatenate([w_in[:, :s2], kr, _rot_cols(kr), w_in[:, s3:]], axis=1).astype(BF16)
    qh = QK_NOPE + QK_ROPE
    cols = []
    for h in range(N_HEADS):
        rope = w_uq[:, h * qh + QK_NOPE:(h + 1) * qh]
        cols += [w_uq[:, h * qh:h * qh + QK_NOPE], rope, _rot_cols(rope)]
    wuq = jnp.concatenate(cols, axis=1).astype(BF16)
    kvh = QK_NOPE + V_HEAD
    kcols = [w_ukv[:, h * kvh:h * kvh + QK_NOPE] for h in range(N_HEADS)]
    vcols = [w_ukv[:, h * kvh + QK_NOPE:(h + 1) * kvh] for h in range(N_HEADS)]
    wukv = jnp.concatenate(kcols + vcols, axis=1).astype(BF16)
    return w1, wuq, wukv


def _trunk(x, meta_tokens, pre_mix_g, w1, q_norm_g, wuq, kv_norm_g, wukv, conv_w,
           f_w1, f_b1, f_freq, f_w2, f_b2, f_w3, f_decay, hy_bias, attn_out_g, hy_out_g,
           wo, post_mix_g, pre_mlp_g, wff1, wff2, post_mlp_g):
    B, S, _ = x.shape
    L = S + N_META
    g = _Geom(L)
    fa, fbf, fbi, ct, st, feats = _tables(L)
    fa, fbf, fbi = (jnp.asarray(a).astype(BF16) for a in (fa, fbf, fbi))
    meta = jnp.broadcast_to(meta_tokens.astype(x.dtype)[None], (B, N_META, D_MODEL))
    hp = jnp.concatenate([meta, x, jnp.zeros((B, g.Lp - L, D_MODEL), x.dtype)], axis=1)

    q, kn, kr, v, u = _proj(hp, pre_mix_g, w1, q_norm_g, wuq, kv_norm_g, wukv, jnp.asarray(ct), jnp.asarray(st))
    o_attn = _attention(q, kn, kr, v, L)
    hf = _filters(jnp.asarray(feats), f_w1, f_b1, f_freq, f_w2, f_b2, f_w3, f_decay, L)
    hspec = _spectrum(hf, fa, fbf, g)
    o_hy = _hyena(u, conv_w, hy_bias, hspec, fa, fbf, fbi, L, g)
    R = B * g.Lp
    return _mlp(x, o_attn.reshape(R, ATTN_WIDTH), o_hy.reshape(R, HYENA_WIDTH),
                attn_out_g, hy_out_g, wo, post_mix_g, pre_mlp_g, wff1, wff2, post_mlp_g)


def kernel(x_prompt, x_sample, meta_tokens, pre_mix_g, w_in, q_norm_g, w_uq, kv_norm_g, w_ukv, conv_w, f_w1, f_b1, f_freq, f_w2, f_b2, f_w3, f_decay, hy_bias, attn_out_g, hy_out_g, w_o, post_mix_g, pre_mlp_g, w_ff1, w_ff2, post_mlp_g):
    assert pre_mix_g.shape[0] == 1, "single-layer trunk"
    w1, wuq, wukv = _prep_weights(w_in[0], w_uq[0], w_ukv[0])
    shared = (meta_tokens, pre_mix_g, w1, q_norm_g, wuq, kv_norm_g, wukv, conv_w[0],
              f_w1[0], f_b1, f_freq, f_w2[0], f_b2, f_w3[0], f_decay, hy_bias[0][:, None, :], attn_out_g, hy_out_g,
              w_o[0].astype(BF16), post_mix_g, pre_mlp_g, w_ff1[0].astype(BF16), w_ff2[0].astype(BF16), post_mlp_g)
    return (_trunk(x_prompt, *shared), _trunk(x_sample, *shared))
```

```python
import functools
import math

import numpy as np
import jax
import jax.numpy as jnp
from jax import lax
from jax.experimental import pallas as pl
from jax.experimental.pallas import tpu as pltpu

F32 = jnp.float32
BF16 = jnp.bfloat16

D_MODEL = 1024
N_META = 16
ATTN_WIDTH = 512
HYENA_WIDTH = 512
N_HEADS = 4
QK_NOPE = 128
QK_ROPE = 64
V_HEAD = 128
Q_LORA = 256
KV_LORA = 128
ROPE_BASE = 10000.0
FILTER_EMB = 33
FILTER_BANDS = 16
HYENA_ORDER = 2
D_FF = 4096
NORM_EPS = 1e-6

LANE = 128
SUBLANE = 8
BF16_ROWS = 16
VMEM_LIMIT = 56 * 1024 * 1024
HYENA_VMEM_BUDGET = 44 * 1024 * 1024
STAGE_A_UNROLL = 32
STAGE_B_UNROLL_CAP = 12
ATTN_KV_CHUNK = 2048
ATTN_VMEM_BUDGET = 46 * 1024 * 1024

Q_SCALE = (QK_NOPE + QK_ROPE) ** -0.5 * math.log2(math.e)


def _round_up(x, m):
    return -(-x // m) * m


def _pick_tile(n, cap, mult):
    best = None
    for d in range(mult, min(n, cap) + 1, mult):
        if n % d == 0:
            best = d
    return n if best is None else best


def _rms(x, g):
    return x * lax.rsqrt(jnp.mean(x * x, axis=-1, keepdims=True) + NORM_EPS) * g


def _const_spec(shape):
    nd = len(shape)
    return pl.BlockSpec(shape, lambda *_: (0,) * nd, pipeline_mode=pl.Buffered(1))


class _Geom:
    def __init__(self, L):
        self.L = L
        self.n1h = -(-L // LANE)
        n1 = -(-(2 * L - 1) // LANE)
        self.n1 = n1 + (n1 % 2)
        self.N = self.n1 * LANE
        self.k1 = self.n1 // 2 + 1
        self.k1p = _round_up(self.k1, SUBLANE)
        self.ka = 2 * self.k1p
        self.n1hp = _round_up(self.n1h, BF16_ROWS)
        self.Lp = self.n1h * LANE
        self.zrows = self.n1hp * LANE


@functools.lru_cache(maxsize=None)
def _tables(L):
    g = _Geom(L)
    n2 = np.arange(LANE)[:, None, None]
    k1 = np.arange(g.k1)[None, :, None]
    n1 = np.arange(g.n1hp)[None, None, :]
    n = n1 * LANE + n2
    ang = 2.0 * np.pi * ((k1 * n) % g.N).astype(np.float64) / g.N
    valid = (n1 < g.n1h)
    fa = np.zeros((LANE, g.ka, g.n1hp), np.float32)
    fa[:, 0 : 2 * g.k1 : 2, :] = np.cos(ang) * valid
    fa[:, 1 : 2 * g.k1 : 2, :] = -np.sin(ang) * valid
    kk = np.arange(LANE)
    th = 2.0 * np.pi * ((kk[:, None] * kk[None, :]) % LANE) / LANE
    c, s = np.cos(th), np.sin(th)
    inter = np.stack([kk, LANE + kk], axis=1).reshape(-1)
    fbf = np.block([[c, s], [-s, c]]).astype(np.float32)[:, inter]
    fbi = np.block([[c, -s], [s, c]]).astype(np.float32)[inter, :]
    inv = 1.0 / (ROPE_BASE ** (np.arange(0, QK_ROPE, 2, dtype=np.float64) / QK_ROPE))
    pos = np.arange(g.Lp, dtype=np.float64)[:, None] * inv[None, :]
    pos = np.concatenate([pos, pos], axis=-1)
    ct = np.zeros((g.Lp, LANE), np.float32)
    st = np.zeros((g.Lp, LANE), np.float32)
    ct[:, :QK_ROPE] = np.cos(pos)
    st[:, :QK_ROPE] = np.sin(pos)
    t = np.linspace(0.0, 1.0, L)[:, None]
    w = 2.0 * np.pi * np.arange(L, dtype=np.float64)[:, None] / L
    f = np.linspace(1e-4, FILTER_BANDS - 1, FILTER_BANDS)[None, :]
    feats = np.zeros((g.Lp, FILTER_EMB), np.float32)
    feats[:L] = np.concatenate([t, np.cos(f * w), -np.sin(f * w)], axis=-1)
    return fa, fbf, fbi, ct, st, feats


def _rope128(x, ct, st):
    return x * ct + pltpu.roll(x, QK_ROPE, 1) * st


def _proj_kernel(h_ref, g_ref, w1_ref, gq_ref, wuq_ref, gkv_ref, wukv_ref, ct_ref, st_ref,
                 q_ref, kn_ref, kr_ref, v_ref, u_ref):
    a = _rms(h_ref[...], g_ref[...]).astype(BF16)
    p = jnp.dot(a, w1_ref[...], preferred_element_type=F32)
    u_ref[...] = p[:, 512:].astype(BF16)
    ct = ct_ref[...]
    st = st_ref[...]
    kr_ref[...] = _rope128(p[:, 384:512], ct, st).astype(BF16)
    cq = _rms(p[:, :Q_LORA], gq_ref[...]).astype(BF16)
    q = jnp.dot(cq, wuq_ref[...], preferred_element_type=F32)
    for h in range(N_HEADS):
        lo = h * 2 * LANE
        q_ref[:, lo:lo + LANE] = (q[:, lo:lo + LANE] * Q_SCALE).astype(BF16)
        q_ref[:, lo + LANE:lo + 2 * LANE] = (_rope128(q[:, lo + LANE:lo + 2 * LANE], ct, st) * Q_SCALE).astype(BF16)
    ckv = _rms(p[:, Q_LORA:Q_LORA + KV_LORA], gkv_ref[...]).astype(BF16)
    kv = jnp.dot(ckv, wukv_ref[...], preferred_element_type=F32)
    kn_ref[...] = kv[:, :ATTN_WIDTH].astype(BF16)
    v_ref[...] = kv[:, ATTN_WIDTH:].astype(BF16)


def _proj(hp, g_pre, w1, gq, wuq, gkv, wukv, ct, st):
    B, Lp, _ = hp.shape
    tm = _pick_tile(Lp, 900, BF16_ROWS)
    row = lambda w: pl.BlockSpec((None, tm, w), lambda b, i: (b, i, 0))
    tab = pl.BlockSpec((tm, LANE), lambda b, i: (i, 0))
    outs = [jax.ShapeDtypeStruct((B, Lp, w), BF16) for w in (4 * 2 * LANE, ATTN_WIDTH, LANE, ATTN_WIDTH, 3 * HYENA_WIDTH)]
    return pl.pallas_call(
        _proj_kernel,
        grid=(B, Lp // tm),
        in_specs=[row(D_MODEL), _const_spec(g_pre.shape), _const_spec(w1.shape), _const_spec(gq.shape),
                  _const_spec(wuq.shape), _const_spec(gkv.shape), _const_spec(wukv.shape), tab, tab],
        out_specs=[row(4 * 2 * LANE), row(ATTN_WIDTH), row(LANE), row(ATTN_WIDTH), row(3 * HYENA_WIDTH)],
        out_shape=outs,
        compiler_params=pltpu.CompilerParams(dimension_semantics=("arbitrary", "arbitrary"), vmem_limit_bytes=VMEM_LIMIT),
        name="proj",
    )(hp, g_pre, w1, gq, wuq, gkv, wukv, ct, st)


def _attn_kernel(q_ref, kn_ref, kr_ref, v_ref, o_ref, *, L, Lp, tk, hps):
    tq = q_ref.shape[0]
    n_full = (Lp - 1) // tk
    last = n_full * tk

    for h in range(hps):
        q = q_ref[:, h * 2 * LANE:(h + 1) * 2 * LANE]
        cols = slice(h * LANE, (h + 1) * LANE)

        def chunk(carry, start, size, masked, q=q, cols=cols):
            m, l, acc = carry
            k = jnp.concatenate([kn_ref[start:start + size, cols], kr_ref[start:start + size, :]], axis=1)
            s = lax.dot_general(q, k, (((1,), (1,)), ((), ())), preferred_element_type=F32)
            if masked:
                col = lax.broadcasted_iota(jnp.int32, (1, size), 1)
                s = jnp.where(col < L - start, s, -1e30)
            m_new = jnp.maximum(m, jnp.max(s, axis=1, keepdims=True))
            alpha = jnp.exp2(m - m_new)
            p = jnp.exp2(s - m_new)
            l = alpha * l + jnp.sum(p, axis=1, keepdims=True)
            acc = alpha * acc + jnp.dot(p.astype(BF16), v_ref[start:start + size, cols], preferred_element_type=F32)
            return m_new, l, acc

        carry = (jnp.full((tq, 1), -1e30, F32), jnp.zeros((tq, 1), F32), jnp.zeros((tq, V_HEAD), F32))
        for i in range(n_full):
            carry = chunk(carry, i * tk, tk, False)
        _, l, acc = chunk(carry, last, Lp - last, True)
        o_ref[:, cols] = (acc / l).astype(BF16)


def _attention(q, kn, kr, v, L):
    B, Lp, _ = q.shape
    tk = min(ATTN_KV_CHUNK, Lp)
    resident = lambda hps: 2 * Lp * (2 * hps + 1) * LANE * 2
    hps = next((h for h in (4, 2) if resident(h) <= ATTN_VMEM_BUDGET // 2), 1)
    tq = max(t for t in range(BF16_ROWS, Lp + 1, BF16_ROWS)
             if Lp % t == 0 and (t == BF16_ROWS or resident(hps) + 4 * t * tk * 4 <= ATTN_VMEM_BUDGET))
    full = lambda w, idx: pl.BlockSpec((None, Lp, w), idx)
    return pl.pallas_call(
        functools.partial(_attn_kernel, L=L, Lp=Lp, tk=tk, hps=hps),
        grid=(B, N_HEADS // hps, Lp // tq),
        in_specs=[pl.BlockSpec((None, tq, hps * 2 * LANE), lambda b, h, i: (b, i, h)),
                  full(hps * LANE, lambda b, h, i: (b, 0, h)), full(LANE, lambda b, h, i: (b, 0, 0)),
                  full(hps * LANE, lambda b, h, i: (b, 0, h))],
        out_specs=pl.BlockSpec((None, tq, hps * V_HEAD), lambda b, h, i: (b, i, h)),
        out_shape=jax.ShapeDtypeStruct((B, Lp, ATTN_WIDTH), BF16),
        compiler_params=pltpu.CompilerParams(dimension_semantics=("arbitrary",) * 3, vmem_limit_bytes=VMEM_LIMIT),
        name="attn",
    )(q, kn, kr, v)


def _filt_kernel(feat_ref, w1_ref, b1_ref, fr_ref, w2_ref, b2_ref, w3_ref, dec_ref, o_ref, *, L, tf):
    hp = lax.Precision.HIGHEST
    f = feat_ref[...]
    fr = fr_ref[...]
    h = jnp.sin(fr * (jnp.dot(f, w1_ref[...], precision=hp, preferred_element_type=F32) + b1_ref[...]))
    h = jnp.sin(fr * (jnp.dot(h, w2_ref[...], precision=hp, preferred_element_type=F32) + b2_ref[...]))
    h = jnp.dot(h, w3_ref[...], precision=hp, preferred_element_type=F32)
    h = h * jnp.exp(-f[:, 0:1] * jnp.abs(dec_ref[...]))
    row = pl.program_id(0) * tf + lax.broadcasted_iota(jnp.int32, (tf, 1), 0)
    col = lax.broadcasted_iota(jnp.int32, (1, h.shape[1]), 1)
    is_bwd = ((col // HYENA_WIDTH) % 2) == 1
    keep = (row < L) & jnp.logical_not(is_bwd & (row == 0))
    o_ref[...] = jnp.where(keep, h, 0.0)


def _filters(feats, f_w1, f_b1, f_freq, f_w2, f_b2, f_w3, f_decay, L):
    Lp = feats.shape[0]
    tf = _pick_tile(Lp, 640, SUBLANE)
    nf = f_w3.shape[1]
    args = (f_w1, f_b1, f_freq, f_w2, f_b2, f_w3, f_decay)
    return pl.pallas_call(
        functools.partial(_filt_kernel, L=L, tf=tf),
        grid=(Lp // tf,),
        in_specs=[pl.BlockSpec((tf, FILTER_EMB), lambda i: (i, 0))] + [_const_spec(a.shape) for a in args],
        out_specs=pl.BlockSpec((tf, nf), lambda i: (i, 0)),
        out_shape=jax.ShapeDtypeStruct((Lp, nf), F32),
        compiler_params=pltpu.CompilerParams(dimension_semantics=("arbitrary",), vmem_limit_bytes=VMEM_LIMIT),
        name="filt",
    )(feats, *args)


def _load_planes(buf, rows):
    return jnp.concatenate([buf[j, rows, :] for j in range(buf.shape[0])], axis=1)


def _store_planes(buf, rows, val):
    for j in range(buf.shape[0]):
        buf[j, rows, :] = val[:, j * LANE:(j + 1) * LANE]


def _pack_rows(x):
    return pltpu.bitcast(x.astype(BF16), jnp.uint32)


def _unpack_rows(x):
    return pltpu.bitcast(x, BF16)


def _stage_a_fwd(zbuf, fa_ref, sbuf, g):
    def body(n2, c):
        x = _load_planes(zbuf, pl.ds(n2, g.n1hp, stride=LANE)).astype(BF16)
        _store_planes(sbuf, pl.ds(pl.multiple_of(n2 * g.k1p, SUBLANE), g.k1p),
                      _pack_rows(jnp.dot(fa_ref[n2], x, preferred_element_type=F32)))
        return c

    lax.fori_loop(0, LANE, body, 0, unroll=STAGE_A_UNROLL)


def _stage_a_inv(sbuf, fa_ref, ybuf, g):
    def body(n2, c):
        s = _unpack_rows(_load_planes(sbuf, pl.ds(pl.multiple_of(n2 * g.k1p, SUBLANE), g.k1p)))
        y = lax.dot_general(fa_ref[n2], s, (((0,), (0,)), ((), ())), preferred_element_type=F32)
        _store_planes(ybuf, pl.ds(n2, g.n1hp, stride=LANE), y)
        return c

    lax.fori_loop(0, LANE, body, 0, unroll=STAGE_A_UNROLL)


def _load_k1(sbuf, k1, g):
    return _unpack_rows(_load_planes(sbuf, pl.ds(k1, LANE, stride=g.k1p)))


def _store_k1(sbuf, k1, g, b):
    _store_planes(sbuf, pl.ds(k1, LANE, stride=g.k1p), _pack_rows(b))


def _stage_b_unroll(g):
    return _pick_tile(g.k1, STAGE_B_UNROLL_CAP, 1)


def _plane_scratch(rows, cb, dtype=F32):
    return pltpu.VMEM((cb // LANE, rows, LANE), dtype)


def _hyena_cb(g, bytes_per_lane):
    fixed = LANE * g.ka * LANE * 2
    for cb in (HYENA_WIDTH, HYENA_WIDTH // 2, LANE):
        if fixed + cb * bytes_per_lane <= HYENA_VMEM_BUDGET:
            return cb
    return LANE


def _spec_kernel(hf_ref, fa_ref, fbf_ref, out_ref, zbuf, sbuf, acc, *, g):
    d = pl.program_id(2)
    _store_planes(zbuf, slice(0, g.Lp), hf_ref[...])
    if g.zrows > g.Lp:
        zbuf[:, g.Lp:, :] = jnp.zeros((zbuf.shape[0], g.zrows - g.Lp, LANE), F32)
    _stage_a_fwd(zbuf, fa_ref, sbuf, g)

    def run(backward):
        def body(k1, c):
            x = jnp.dot(fbf_ref[...], _load_k1(sbuf, k1, g), preferred_element_type=F32)
            scale = jnp.where((k1 == 0) | (k1 == g.n1 // 2), 1.0, 2.0) / g.N
            xr = x[:LANE] * scale
            xi = x[LANE:] * scale
            if backward:
                out_ref[k1, 0] = (acc[k1, 0] + xr).astype(BF16)
                out_ref[k1, 1] = (acc[k1, 1] - xi).astype(BF16)
            else:
                acc[k1, 0] = xr
                acc[k1, 1] = xi
            return c

        lax.fori_loop(0, g.k1, body, 0, unroll=_stage_b_unroll(g))

    pl.when(d == 0)(lambda: run(False))
    pl.when(d == 1)(lambda: run(True))


def _spectrum(hf, fa, fbf, g):
    cb = _hyena_cb(g, 4 * (g.zrows + LANE * g.k1p + g.k1 * 2 * LANE) + 2 * 4 * g.Lp + 2 * 2 * g.k1 * 2 * LANE)
    ncb = HYENA_WIDTH // cb
    return pl.pallas_call(
        functools.partial(_spec_kernel, g=g),
        grid=(HYENA_ORDER, ncb, 2),
        in_specs=[pl.BlockSpec((g.Lp, cb), lambda o, c, d: (0, (o * 2 + d) * ncb + c)),
                  _const_spec(fa.shape), _const_spec(fbf.shape)],
        out_specs=pl.BlockSpec((None, g.k1, 2, LANE, cb), lambda o, c, d: (o, 0, 0, 0, c)),
        out_shape=jax.ShapeDtypeStruct((HYENA_ORDER, g.k1, 2, LANE, HYENA_WIDTH), BF16),
        scratch_shapes=[_plane_scratch(g.zrows, cb), _plane_scratch(LANE * g.k1p, cb, jnp.uint32),
                        pltpu.VMEM((g.k1, 2, LANE, cb), F32)],
        compiler_params=pltpu.CompilerParams(dimension_semantics=("arbitrary",) * 3, vmem_limit_bytes=VMEM_LIMIT),
        name="spec",
    )(hf, fa, fbf)


def _short_conv(u_ref, w_ref, r0, t, Lp):
    halo = BF16_ROWS
    cb = u_ref.shape[1]
    parts = []
    if r0 == 0:
        parts.append(jnp.zeros((halo, cb), F32))
    lo = max(r0 - halo, 0)
    hi = min(r0 + t + halo, Lp)
    parts.append(u_ref[lo:hi, :].astype(F32))
    if r0 + t == Lp:
        parts.append(jnp.zeros((halo, cb), F32))
    x = jnp.concatenate(parts, axis=0) if len(parts) > 1 else parts[0]
    n = t + 2 * halo
    xm1 = pltpu.roll(x, 1, 0)[halo:halo + t]
    xp1 = pltpu.roll(x, n - 1, 0)[halo:halo + t]
    return xm1 * w_ref[0:1, :] + x[halo:halo + t] * w_ref[1:2, :] + xp1 * w_ref[2:3, :]


def _hyena_kernel(gate_ref, v_ref, wg_ref, wv_ref, bias_ref, h_ref, fa_ref, fbf_ref, fbi_ref,
                  out_ref, zbuf, ybuf, sbuf, *, L, g, t):
    order = pl.program_id(2)
    Lp = g.Lp
    nchunk = Lp // t

    def row_mask(r0):
        return (r0 + lax.broadcasted_iota(jnp.int32, (t, 1), 0)) < L

    @pl.when(order == 0)
    def _():
        if g.zrows > Lp:
            zbuf[:, Lp:, :] = jnp.zeros((zbuf.shape[0], g.zrows - Lp, LANE), F32)
        for c in range(nchunk):
            r0 = c * t
            _store_planes(zbuf, slice(r0, r0 + t),
                          jnp.where(row_mask(r0), _short_conv(v_ref, wv_ref, r0, t, Lp), 0.0))

    _stage_a_fwd(zbuf, fa_ref, sbuf, g)

    def body(k1, c):
        x = jnp.dot(fbf_ref[...], _load_k1(sbuf, k1, g), preferred_element_type=F32)
        xr, xi = x[:LANE], x[LANE:]
        hr = h_ref[k1, 0].astype(F32)
        hi = h_ref[k1, 1].astype(F32)
        y = jnp.concatenate([xr * hr - xi * hi, xr * hi + xi * hr], axis=0).astype(BF16)
        _store_k1(sbuf, k1, g, jnp.dot(fbi_ref[...], y, preferred_element_type=F32))
        return c

    lax.fori_loop(0, g.k1, body, 0, unroll=_stage_b_unroll(g))
    _stage_a_inv(sbuf, fa_ref, ybuf, g)

    bias = bias_ref[...]
    for c in range(nchunk):
        r0 = c * t
        gate = _short_conv(gate_ref, wg_ref, r0, t, Lp)
        rows = slice(r0, r0 + t)
        zn = jnp.where(row_mask(r0), gate * (_load_planes(ybuf, rows) + bias * _load_planes(zbuf, rows)), 0.0)
        _store_planes(zbuf, rows, zn)
        out_ref[rows, :] = zn.astype(BF16)


def _hyena(u, conv_w, hy_bias, hspec, fa, fbf, fbi, L, g):
    B, Lp, _ = u.shape
    cb = _hyena_cb(g, 4 * (2 * g.zrows + LANE * g.k1p) + 2 * 2 * g.k1 * 2 * LANE + 6 * 2 * Lp)
    ncb = HYENA_WIDTH // cb
    t = _pick_tile(Lp, 900 * LANE // cb, BF16_ROWS)
    col = lambda idx: pl.BlockSpec((None, Lp, cb), idx)
    wcol = lambda idx: pl.BlockSpec((3, cb), idx)
    return pl.pallas_call(
        functools.partial(_hyena_kernel, L=L, g=g, t=t),
        grid=(ncb, B, HYENA_ORDER),
        in_specs=[col(lambda c, b, o: (b, 0, o * ncb + c)), col(lambda c, b, o: (b, 0, HYENA_ORDER * ncb + c)),
                  wcol(lambda c, b, o: (0, o * ncb + c)), wcol(lambda c, b, o: (0, HYENA_ORDER * ncb + c)),
                  pl.BlockSpec((None, 1, cb), lambda c, b, o: (o, 0, c)),
                  pl.BlockSpec((None, g.k1, 2, LANE, cb), lambda c, b, o: (o, 0, 0, 0, c)),
                  _const_spec(fa.shape), _const_spec(fbf.shape), _const_spec(fbi.shape)],
        out_specs=col(lambda c, b, o: (b, 0, c)),
        out_shape=jax.ShapeDtypeStruct((B, Lp, HYENA_WIDTH), BF16),
        scratch_shapes=[_plane_scratch(g.zrows, cb), _plane_scratch(g.zrows, cb),
                        _plane_scratch(LANE * g.k1p, cb, jnp.uint32)],
        compiler_params=pltpu.CompilerParams(dimension_semantics=("arbitrary",) * 3, vmem_limit_bytes=VMEM_LIMIT),
        name="hyena",
    )(u, u, conv_w, conv_w, hy_bias, hspec, fa, fbf, fbi)


def _mlp_kernel(h_ref, oa_ref, oh_ref, ga_ref, gh_ref, wo_ref, gpost_ref, gpre_ref, w1_ref, w2_ref, gmlp_ref, o_ref):
    oa = _rms(oa_ref[...].astype(F32), ga_ref[...]).astype(BF16)
    oh = _rms(oh_ref[...].astype(F32), gh_ref[...]).astype(BF16)
    mix = jnp.dot(jnp.concatenate([oa, oh], axis=1), wo_ref[...], preferred_element_type=F32)
    h1 = h_ref[...] + _rms(mix, gpost_ref[...])
    m = jnp.dot(_rms(h1, gpre_ref[...]).astype(BF16), w1_ref[...], preferred_element_type=F32)
    m = jnp.square(jnp.maximum(m, 0.0)).astype(BF16)
    m = jnp.dot(m, w2_ref[...], preferred_element_type=F32)
    o_ref[...] = h1 + _rms(m, gmlp_ref[...])


def _mlp(x, oa, oh, ga, gh, wo, gpost, gpre, w1, w2, gmlp):
    B, S, _ = x.shape
    Lp = oa.shape[0] // B
    tm = _pick_tile(S, 512, BF16_ROWS)
    row = pl.BlockSpec((None, tm, D_MODEL), lambda b, i: (b, i, 0))
    shifted = lambda w: pl.BlockSpec((pl.Element(tm), pl.Element(w)),
                                     lambda b, i: (pl.multiple_of(b * Lp + N_META + i * tm, BF16_ROWS), 0))
    consts = (ga, gh, wo, gpost, gpre, w1, w2, gmlp)
    return pl.pallas_call(
        _mlp_kernel,
        grid=(B, S // tm),
        in_specs=[row, shifted(ATTN_WIDTH), shifted(HYENA_WIDTH)] + [_const_spec(a.shape) for a in consts],
        out_specs=row,
        out_shape=jax.ShapeDtypeStruct((B, S, D_MODEL), F32),
        compiler_params=pltpu.CompilerParams(dimension_semantics=("arbitrary",) * 2, vmem_limit_bytes=VMEM_LIMIT),
        name="mlp",
    )(x, oa, oh, *consts)


def _rot_cols(w):
    half = QK_ROPE // 2
    return jnp.concatenate([-w[:, half:], w[:, :half]], axis=1)


def _prep_weights(w_in, w_uq, w_ukv):
    s1, s2, s3 = Q_LORA, Q_LORA + KV_LORA, Q_LORA + KV_LORA + QK_ROPE
    kr = w_in[:, s2:s3]
    w1 = jnp.concatenate([w_in[:, :s2], kr, _rot_cols(kr), w_in[:, s3:]], axis=1).astype(BF16)
    qh = QK_NOPE + QK_ROPE
    cols = []
    for h in range(N_HEADS):
        rope = w_uq[:, h * qh + QK_NOPE:(h + 1) * qh]
        cols += [w_uq[:, h * qh:h * qh + QK_NOPE], rope, _rot_cols(rope)]
    wuq = jnp.concatenate(cols, axis=1).astype(BF16)
    kvh = QK_NOPE + V_HEAD
    kcols = [w_ukv[:, h * kvh:h * kvh + QK_NOPE] for h in range(N_HEADS)]
    vcols = [w_ukv[:, h * kvh + QK_NOPE:(h + 1) * kvh] for h in range(N_HEADS)]
    wukv = jnp.concatenate(kcols + vcols, axis=1).astype(BF16)
    return w1, wuq, wukv


def _trunk(x, meta_tokens, pre_mix_g, w1, q_norm_g, wuq, kv_norm_g, wukv, conv_w,
           f_w1, f_b1, f_freq, f_w2, f_b2, f_w3, f_decay, hy_bias, attn_out_g, hy_out_g,
           wo, post_mix_g, pre_mlp_g, wff1, wff2, post_mlp_g):
    B, S, _ = x.shape
    L = S + N_META
    g = _Geom(L)
    fa, fbf, fbi, ct, st, feats = _tables(L)
    fa, fbf, fbi = (jnp.asarray(a).astype(BF16) for a in (fa, fbf, fbi))
    meta = jnp.broadcast_to(meta_tokens.astype(x.dtype)[None], (B, N_META, D_MODEL))
    hp = jnp.concatenate([meta, x, jnp.zeros((B, g.Lp - L, D_MODEL), x.dtype)], axis=1)

    q, kn, kr, v, u = _proj(hp, pre_mix_g, w1, q_norm_g, wuq, kv_norm_g, wukv, jnp.asarray(ct), jnp.asarray(st))
    o_attn = _attention(q, kn, kr, v, L)
    hf = _filters(jnp.asarray(feats), f_w1, f_b1, f_freq, f_w2, f_b2, f_w3, f_decay, L)
    hspec = _spectrum(hf, fa, fbf, g)
    o_hy = _hyena(u, conv_w, hy_bias, hspec, fa, fbf, fbi, L, g)
    R = B * g.Lp
    return _mlp(x, o_attn.reshape(R, ATTN_WIDTH), o_hy.reshape(R, HYENA_WIDTH),
                attn_out_g, hy_out_g, wo, post_mix_g, pre_mlp_g, wff1, wff2, post_mlp_g)


def kernel(x_prompt, x_sample, meta_tokens, pre_mix_g, w_in, q_norm_g, w_uq, kv_norm_g, w_ukv, conv_w, f_w1, f_b1, f_freq, f_w2, f_b2, f_w3, f_decay, hy_bias, attn_out_g, hy_out_g, w_o, post_mix_g, pre_mlp_g, w_ff1, w_ff2, post_mlp_g):
    assert pre_mix_g.shape[0] == 1, "single-layer trunk"
    w1, wuq, wukv = _prep_weights(w_in[0], w_uq[0], w_ukv[0])
    shared = (meta_tokens, pre_mix_g, w1, q_norm_g, wuq, kv_norm_g, wukv, conv_w[0],
              f_w1[0], f_b1, f_freq, f_w2[0], f_b2, f_w3[0], f_decay, hy_bias[0][:, None, :], attn_out_g, hy_out_g,
              w_o[0].astype(BF16), post_mix_g, pre_mlp_g, w_ff1[0].astype(BF16), w_ff2[0].astype(BF16), post_mlp_g)
    return (_trunk(x_prompt, *shared), _trunk(x_sample, *shared))
```

```python
import functools
import math

import numpy as np
import jax
import jax.numpy as jnp
from jax import lax
from jax.experimental import pallas as pl
from jax.experimental.pallas import tpu as pltpu

F32 = jnp.float32
BF16 = jnp.bfloat16

D_MODEL = 1024
N_META = 16
ATTN_WIDTH = 512
HYENA_WIDTH = 512
N_HEADS = 4
QK_NOPE = 128
QK_ROPE = 64
V_HEAD = 128
Q_LORA = 256
KV_LORA = 128
ROPE_BASE = 10000.0
FILTER_EMB = 33
FILTER_BANDS = 16
HYENA_ORDER = 2
D_FF = 4096
NORM_EPS = 1e-6

LANE = 128
SUBLANE = 8
BF16_ROWS = 16
VMEM_LIMIT = 56 * 1024 * 1024
HYENA_VMEM_BUDGET = 44 * 1024 * 1024
STAGE_A_UNROLL = 64
STAGE_B_UNROLL_CAP = 22
ATTN_KV_CHUNK = 2048
ATTN_VMEM_BUDGET = 46 * 1024 * 1024

Q_SCALE = (QK_NOPE + QK_ROPE) ** -0.5 * math.log2(math.e)


def _round_up(x, m):
    return -(-x // m) * m


def _pick_tile(n, cap, mult):
    best = None
    for d in range(mult, min(n, cap) + 1, mult):
        if n % d == 0:
            best = d
    return n if best is None else best


def _rms(x, g):
    return x * lax.rsqrt(jnp.mean(x * x, axis=-1, keepdims=True) + NORM_EPS) * g


def _const_spec(shape):
    nd = len(shape)
    return pl.BlockSpec(shape, lambda *_: (0,) * nd, pipeline_mode=pl.Buffered(1))


class _Geom:
    def __init__(self, L):
        self.L = L
        self.n1h = -(-L // LANE)
        n1 = -(-(2 * L - 1) // LANE)
        self.n1 = n1 + (n1 % 2)
        self.N = self.n1 * LANE
        self.k1 = self.n1 // 2 + 1
        self.k1p = _round_up(self.k1, SUBLANE)
        self.ka = 2 * self.k1p
        self.n1hp = _round_up(self.n1h, BF16_ROWS)
        self.Lp = self.n1h * LANE
        self.zrows = self.n1hp * LANE


@functools.lru_cache(maxsize=None)
def _tables(L):
    g = _Geom(L)
    n2 = np.arange(LANE)[:, None, None]
    k1 = np.arange(g.k1)[None, :, None]
    n1 = np.arange(g.n1hp)[None, None, :]
    n = n1 * LANE + n2
    ang = 2.0 * np.pi * ((k1 * n) % g.N).astype(np.float64) / g.N
    valid = (n1 < g.n1h)
    fa = np.zeros((LANE, g.ka, g.n1hp), np.float32)
    fa[:, 0 : 2 * g.k1 : 2, :] = np.cos(ang) * valid
    fa[:, 1 : 2 * g.k1 : 2, :] = -np.sin(ang) * valid
    kk = np.arange(LANE)
    th = 2.0 * np.pi * ((kk[:, None] * kk[None, :]) % LANE) / LANE
    c, s = np.cos(th), np.sin(th)
    inter = np.stack([kk, LANE + kk], axis=1).reshape(-1)
    fbf = np.block([[c, s], [-s, c]]).astype(np.float32)[:, inter]
    fbi = np.block([[c, -s], [s, c]]).astype(np.float32)[inter, :]
    inv = 1.0 / (ROPE_BASE ** (np.arange(0, QK_ROPE, 2, dtype=np.float64) / QK_ROPE))
    pos = np.arange(g.Lp, dtype=np.float64)[:, None] * inv[None, :]
    pos = np.concatenate([pos, pos], axis=-1)
    ct = np.zeros((g.Lp, LANE), np.float32)
    st = np.zeros((g.Lp, LANE), np.float32)
    ct[:, :QK_ROPE] = np.cos(pos)
    st[:, :QK_ROPE] = np.sin(pos)
    t = np.linspace(0.0, 1.0, L)[:, None]
    w = 2.0 * np.pi * np.arange(L, dtype=np.float64)[:, None] / L
    f = np.linspace(1e-4, FILTER_BANDS - 1, FILTER_BANDS)[None, :]
    feats = np.zeros((g.Lp, FILTER_EMB), np.float32)
    feats[:L] = np.concatenate([t, np.cos(f * w), -np.sin(f * w)], axis=-1)
    return fa, fbf, fbi, ct, st, feats


def _rope128(x, ct, st):
    return x * ct + pltpu.roll(x, QK_ROPE, 1) * st


def _proj_kernel(x_ref, halo_ref, meta_ref, g_ref, w1_ref, gq_ref, wuq_ref, gkv_ref, wukv_ref, ct_ref, st_ref,
                 q_ref, kn_ref, kr_ref, v_ref, u_ref, a_scr, *, tm, n, S):
    i = pl.program_id(1)
    bm = x_ref.shape[0]
    norm = lambda rows: _rms(rows, g_ref[...]).astype(BF16)
    zeros = lambda r: jnp.zeros((r, D_MODEL), BF16)

    @pl.when(i == 0)
    def _():
        take = min(tm - N_META, S)
        a_scr[0:N_META, :] = norm(meta_ref[...])
        a_scr[N_META:N_META + take, :] = norm(x_ref[0:take, :])
        if N_META + take < tm:
            a_scr[N_META + take:, :] = zeros(tm - N_META - take)

    if n > 2:
        @pl.when((i > 0) & (i < n - 1))
        def _():
            a_scr[0:N_META, :] = norm(halo_ref[...])
            a_scr[N_META:, :] = norm(x_ref[0:tm - N_META, :])

    if n > 1:
        @pl.when(i == n - 1)
        def _():
            skip = n * tm - S - N_META
            a_scr[0:bm - skip, :] = norm(x_ref[skip:, :])
            a_scr[bm - skip:, :] = zeros(tm - bm + skip)

    p = jnp.dot(a_scr[...], w1_ref[...], preferred_element_type=F32)
    u_ref[...] = p[:, 512:].astype(BF16)
    ct = ct_ref[...]
    st = st_ref[...]
    kr_ref[...] = _rope128(p[:, 384:512], ct, st).astype(BF16)
    cq = _rms(p[:, :Q_LORA], gq_ref[...]).astype(BF16)
    q = jnp.dot(cq, wuq_ref[...], preferred_element_type=F32)
    for h in range(N_HEADS):
        lo = h * 2 * LANE
        q_ref[:, lo:lo + LANE] = (q[:, lo:lo + LANE] * Q_SCALE).astype(BF16)
        q_ref[:, lo + LANE:lo + 2 * LANE] = (_rope128(q[:, lo + LANE:lo + 2 * LANE], ct, st) * Q_SCALE).astype(BF16)
    ckv = _rms(p[:, Q_LORA:Q_LORA + KV_LORA], gkv_ref[...]).astype(BF16)
    kv = jnp.dot(ckv, wukv_ref[...], preferred_element_type=F32)
    kn_ref[...] = kv[:, :ATTN_WIDTH].astype(BF16)
    v_ref[...] = kv[:, ATTN_WIDTH:].astype(BF16)


def _proj(x, meta, g_pre, w1, gq, wuq, gkv, wukv, ct, st):
    B, S, _ = x.shape
    Lp = ct.shape[0]
    tm = _pick_tile(Lp, 900, BF16_ROWS)
    n = Lp // tm
    bm = min(tm, S)
    assert n == 1 or (bm == tm and n * tm - S >= N_META)
    x_rows = pl.BlockSpec((pl.Element(bm), pl.Element(D_MODEL)),
                          lambda b, i: (pl.multiple_of(b * S + jnp.minimum(i * tm, S - bm), SUBLANE), 0))
    halo = pl.BlockSpec((pl.Element(N_META), pl.Element(D_MODEL)),
                        lambda b, i: (pl.multiple_of(b * S + jnp.maximum(i * tm - N_META, 0), SUBLANE), 0))
    row = lambda w: pl.BlockSpec((None, tm, w), lambda b, i: (b, i, 0))
    tab = pl.BlockSpec((tm, LANE), lambda b, i: (i, 0))
    outs = [jax.ShapeDtypeStruct((B, Lp, w), BF16) for w in (4 * 2 * LANE, ATTN_WIDTH, LANE, ATTN_WIDTH, 3 * HYENA_WIDTH)]
    x2 = x.reshape(B * S, D_MODEL)
    return pl.pallas_call(
        functools.partial(_proj_kernel, tm=tm, n=n, S=S),
        grid=(B, n),
        in_specs=[x_rows, halo, _const_spec(meta.shape), _const_spec(g_pre.shape), _const_spec(w1.shape),
                  _const_spec(gq.shape), _const_spec(wuq.shape), _const_spec(gkv.shape), _const_spec(wukv.shape), tab, tab],
        out_specs=[row(4 * 2 * LANE), row(ATTN_WIDTH), row(LANE), row(ATTN_WIDTH), row(3 * HYENA_WIDTH)],
        out_shape=outs,
        scratch_shapes=[pltpu.VMEM((tm, D_MODEL), BF16)],
        compiler_params=pltpu.CompilerParams(dimension_semantics=("arbitrary", "arbitrary"), vmem_limit_bytes=VMEM_LIMIT),
        name="proj",
    )(x2, x2, meta, g_pre, w1, gq, wuq, gkv, wukv, ct, st)


def _attn_kernel(q_ref, kn_ref, kr_ref, v_ref, o_ref, *, L, Lp, tk, hps):
    tq = q_ref.shape[0]
    n_full = (Lp - 1) // tk
    last = n_full * tk

    for h in range(hps):
        q = q_ref[:, h * 2 * LANE:(h + 1) * 2 * LANE]
        cols = slice(h * LANE, (h + 1) * LANE)

        def chunk(carry, start, size, masked, q=q, cols=cols):
            m, l, acc = carry
            k = jnp.concatenate([kn_ref[start:start + size, cols], kr_ref[start:start + size, :]], axis=1)
            s = lax.dot_general(q, k, (((1,), (1,)), ((), ())), preferred_element_type=F32)
            if masked:
                col = lax.broadcasted_iota(jnp.int32, (1, size), 1)
                s = jnp.where(col < L - start, s, -1e30)
            m_new = jnp.maximum(m, jnp.max(s, axis=1, keepdims=True))
            alpha = jnp.exp2(m - m_new)
            p = jnp.exp2(s - m_new)
            l = alpha * l + jnp.sum(p, axis=1, keepdims=True)
            acc = alpha * acc + jnp.dot(p.astype(BF16), v_ref[start:start + size, cols], preferred_element_type=F32)
            return m_new, l, acc

        carry = (jnp.full((tq, 1), -1e30, F32), jnp.zeros((tq, 1), F32), jnp.zeros((tq, V_HEAD), F32))
        for i in range(n_full):
            carry = chunk(carry, i * tk, tk, False)
        _, l, acc = chunk(carry, last, Lp - last, True)
        o_ref[:, cols] = (acc / l).astype(BF16)


def _attention(q, kn, kr, v, L):
    B, Lp, _ = q.shape
    tk = min(ATTN_KV_CHUNK, Lp)
    resident = lambda hps: 2 * Lp * (2 * hps + 1) * LANE * 2
    hps = next((h for h in (4, 2) if resident(h) <= ATTN_VMEM_BUDGET // 2), 1)
    tq = max(t for t in range(BF16_ROWS, Lp + 1, BF16_ROWS)
             if Lp % t == 0 and (t == BF16_ROWS or resident(hps) + 4 * t * tk * 4 <= ATTN_VMEM_BUDGET))
    full = lambda w, idx: pl.BlockSpec((None, Lp, w), idx)
    return pl.pallas_call(
        functools.partial(_attn_kernel, L=L, Lp=Lp, tk=tk, hps=hps),
        grid=(B, N_HEADS // hps, Lp // tq),
        in_specs=[pl.BlockSpec((None, tq, hps * 2 * LANE), lambda b, h, i: (b, i, h)),
                  full(hps * LANE, lambda b, h, i: (b, 0, h)), full(LANE, lambda b, h, i: (b, 0, 0)),
                  full(hps * LANE, lambda b, h, i: (b, 0, h))],
        out_specs=pl.BlockSpec((None, tq, hps * V_HEAD), lambda b, h, i: (b, i, h)),
        out_shape=jax.ShapeDtypeStruct((B, Lp, ATTN_WIDTH), BF16),
        compiler_params=pltpu.CompilerParams(dimension_semantics=("arbitrary",) * 3, vmem_limit_bytes=VMEM_LIMIT),
        name="attn",
    )(q, kn, kr, v)


def _filt_kernel(feat_ref, w1_ref, b1_ref, fr_ref, w2_ref, b2_ref, w3_ref, dec_ref, o_ref, *, L, tf):
    hp = lax.Precision.HIGHEST
    f = feat_ref[...]
    fr = fr_ref[...]
    h = jnp.sin(fr * (jnp.dot(f, w1_ref[...], precision=hp, preferred_element_type=F32) + b1_ref[...]))
    h = jnp.sin(fr * (jnp.dot(h, w2_ref[...], precision=hp, preferred_element_type=F32) + b2_ref[...]))
    h = jnp.dot(h, w3_ref[...], precision=hp, preferred_element_type=F32)
    h = h * jnp.exp(-f[:, 0:1] * jnp.abs(dec_ref[...]))
    row = pl.program_id(0) * tf + lax.broadcasted_iota(jnp.int32, (tf, 1), 0)
    col = lax.broadcasted_iota(jnp.int32, (1, h.shape[1]), 1)
    is_bwd = ((col // HYENA_WIDTH) % 2) == 1
    keep = (row < L) & jnp.logical_not(is_bwd & (row == 0))
    o_ref[...] = jnp.where(keep, h, 0.0)


def _filters(feats, f_w1, f_b1, f_freq, f_w2, f_b2, f_w3, f_decay, L):
    Lp = feats.shape[0]
    tf = _pick_tile(Lp, 640, SUBLANE)
    nf = f_w3.shape[1]
    args = (f_w1, f_b1, f_freq, f_w2, f_b2, f_w3, f_decay)
    return pl.pallas_call(
        functools.partial(_filt_kernel, L=L, tf=tf),
        grid=(Lp // tf,),
        in_specs=[pl.BlockSpec((tf, FILTER_EMB), lambda i: (i, 0))] + [_const_spec(a.shape) for a in args],
        out_specs=pl.BlockSpec((tf, nf), lambda i: (i, 0)),
        out_shape=jax.ShapeDtypeStruct((Lp, nf), F32),
        compiler_params=pltpu.CompilerParams(dimension_semantics=("arbitrary",), vmem_limit_bytes=VMEM_LIMIT),
        name="filt",
    )(feats, *args)


def _load_planes(buf, rows):
    return jnp.concatenate([buf[j, rows, :] for j in range(buf.shape[0])], axis=1)


def _store_planes(buf, rows, val):
    for j in range(buf.shape[0]):
        buf[j, rows, :] = val[:, j * LANE:(j + 1) * LANE]


def _pack_rows(x):
    return pltpu.bitcast(x.astype(BF16), jnp.uint32)


def _unpack_rows(x):
    return pltpu.bitcast(x, BF16)


def _stage_a_fwd(zbuf, fa_ref, sbuf, g):
    def body(n2, c):
        x = _load_planes(zbuf, pl.ds(n2, g.n1hp, stride=LANE)).astype(BF16)
        _store_planes(sbuf, pl.ds(pl.multiple_of(n2 * g.k1p, SUBLANE), g.k1p),
                      _pack_rows(jnp.dot(fa_ref[n2], x, preferred_element_type=F32)))
        return c

    lax.fori_loop(0, LANE, body, 0, unroll=STAGE_A_UNROLL)


def _stage_a_inv(sbuf, fa_ref, ybuf, g):
    def body(n2, c):
        s = _unpack_rows(_load_planes(sbuf, pl.ds(pl.multiple_of(n2 * g.k1p, SUBLANE), g.k1p)))
        y = lax.dot_general(fa_ref[n2], s, (((0,), (0,)), ((), ())), preferred_element_type=F32)
        _store_planes(ybuf, pl.ds(n2, g.n1hp, stride=LANE), y)
        return c

    lax.fori_loop(0, LANE, body, 0, unroll=STAGE_A_UNROLL)


def _load_k1(sbuf, k1, g):
    return _unpack_rows(_load_planes(sbuf, pl.ds(k1, LANE, stride=g.k1p)))


def _store_k1(sbuf, k1, g, b):
    _store_planes(sbuf, pl.ds(k1, LANE, stride=g.k1p), _pack_rows(b))


def _stage_b_unroll(g):
    return _pick_tile(g.k1, STAGE_B_UNROLL_CAP, 1)


def _plane_scratch(rows, cb, dtype=F32):
    return pltpu.VMEM((cb // LANE, rows, LANE), dtype)


def _hyena_cb(g, bytes_per_lane):
    fixed = LANE * g.ka * LANE * 2
    for cb in (HYENA_WIDTH, HYENA_WIDTH // 2, LANE):
        if fixed + cb * bytes_per_lane <= HYENA_VMEM_BUDGET:
            return cb
    return LANE


def _spec_kernel(hf_ref, fa_ref, fbf_ref, out_ref, zbuf, sbuf, acc, *, g):
    d = pl.program_id(2)
    _store_planes(zbuf, slice(0, g.Lp), hf_ref[...])
    if g.zrows > g.Lp:
        zbuf[:, g.Lp:, :] = jnp.zeros((zbuf.shape[0], g.zrows - g.Lp, LANE), F32)
    _stage_a_fwd(zbuf, fa_ref, sbuf, g)

    def run(backward):
        def body(k1, c):
            x = jnp.dot(fbf_ref[...], _load_k1(sbuf, k1, g), preferred_element_type=F32)
            scale = jnp.where((k1 == 0) | (k1 == g.n1 // 2), 1.0, 2.0) / g.N
            xr = x[:LANE] * scale
            xi = x[LANE:] * scale
            if backward:
                out_ref[k1, 0] = (acc[k1, 0] + xr).astype(BF16)
                out_ref[k1, 1] = (acc[k1, 1] - xi).astype(BF16)
            else:
                acc[k1, 0] = xr
                acc[k1, 1] = xi
            return c

        lax.fori_loop(0, g.k1, body, 0, unroll=_stage_b_unroll(g))

    pl.when(d == 0)(lambda: run(False))
    pl.when(d == 1)(lambda: run(True))


def _spectrum(hf, fa, fbf, g):
    cb = _hyena_cb(g, 4 * (g.zrows + LANE * g.k1p + g.k1 * 2 * LANE) + 2 * 4 * g.Lp + 2 * 2 * g.k1 * 2 * LANE)
    ncb = HYENA_WIDTH // cb
    return pl.pallas_call(
        functools.partial(_spec_kernel, g=g),
        grid=(HYENA_ORDER, ncb, 2),
        in_specs=[pl.BlockSpec((g.Lp, cb), lambda o, c, d: (0, (o * 2 + d) * ncb + c)),
                  _const_spec(fa.shape), _const_spec(fbf.shape)],
        out_specs=pl.BlockSpec((None, g.k1, 2, LANE, cb), lambda o, c, d: (o, 0, 0, 0, c)),
        out_shape=jax.ShapeDtypeStruct((HYENA_ORDER, g.k1, 2, LANE, HYENA_WIDTH), BF16),
        scratch_shapes=[_plane_scratch(g.zrows, cb), _plane_scratch(LANE * g.k1p, cb, jnp.uint32),
                        pltpu.VMEM((g.k1, 2, LANE, cb), F32)],
        compiler_params=pltpu.CompilerParams(dimension_semantics=("arbitrary",) * 3, vmem_limit_bytes=VMEM_LIMIT),
        name="spec",
    )(hf, fa, fbf)


def _short_conv(u_ref, w_ref, r0, t, Lp):
    halo = BF16_ROWS
    cb = u_ref.shape[1]
    parts = []
    if r0 == 0:
        parts.append(jnp.zeros((halo, cb), F32))
    lo = max(r0 - halo, 0)
    hi = min(r0 + t + halo, Lp)
    parts.append(u_ref[lo:hi, :].astype(F32))
    if r0 + t == Lp:
        parts.append(jnp.zeros((halo, cb), F32))
    x = jnp.concatenate(parts, axis=0) if len(parts) > 1 else parts[0]
    n = t + 2 * halo
    xm1 = pltpu.roll(x, 1, 0)[halo:halo + t]
    xp1 = pltpu.roll(x, n - 1, 0)[halo:halo + t]
    return xm1 * w_ref[0:1, :] + x[halo:halo + t] * w_ref[1:2, :] + xp1 * w_ref[2:3, :]


def _hyena_kernel(gate_ref, v_ref, wg_ref, wv_ref, bias_ref, h_ref, fa_ref, fbf_ref, fbi_ref,
                  out_ref, zbuf, ybuf, sbuf, *, L, g, t):
    order = pl.program_id(2)
    Lp = g.Lp
    nchunk = Lp // t

    def row_mask(r0):
        return (r0 + lax.broadcasted_iota(jnp.int32, (t, 1), 0)) < L

    @pl.when(order == 0)
    def _():
        if g.zrows > Lp:
            zbuf[:, Lp:, :] = jnp.zeros((zbuf.shape[0], g.zrows - Lp, LANE), F32)
        for c in range(nchunk):
            r0 = c * t
            _store_planes(zbuf, slice(r0, r0 + t),
                          jnp.where(row_mask(r0), _short_conv(v_ref, wv_ref, r0, t, Lp), 0.0))

    _stage_a_fwd(zbuf, fa_ref, sbuf, g)

    def body(k1, c):
        x = jnp.dot(fbf_ref[...], _load_k1(sbuf, k1, g), preferred_element_type=F32)
        xr, xi = x[:LANE], x[LANE:]
        hr = h_ref[k1, 0].astype(F32)
        hi = h_ref[k1, 1].astype(F32)
        y = jnp.concatenate([xr * hr - xi * hi, xr * hi + xi * hr], axis=0).astype(BF16)
        _store_k1(sbuf, k1, g, jnp.dot(fbi_ref[...], y, preferred_element_type=F32))
        return c

    lax.fori_loop(0, g.k1, body, 0, unroll=_stage_b_unroll(g))
    _stage_a_inv(sbuf, fa_ref, ybuf, g)

    bias = bias_ref[...]
    for c in range(nchunk):
        r0 = c * t
        gate = _short_conv(gate_ref, wg_ref, r0, t, Lp)
        rows = slice(r0, r0 + t)
        zn = jnp.where(row_mask(r0), gate * (_load_planes(ybuf, rows) + bias * _load_planes(zbuf, rows)), 0.0)
        _store_planes(zbuf, rows, zn)
        out_ref[rows, :] = zn.astype(BF16)


def _hyena(u, conv_w, hy_bias, hspec, fa, fbf, fbi, L, g):
    B, Lp, _ = u.shape
    cb = _hyena_cb(g, 4 * (2 * g.zrows + LANE * g.k1p) + 2 * 2 * g.k1 * 2 * LANE + 6 * 2 * Lp)
    ncb = HYENA_WIDTH // cb
    t = _pick_tile(Lp, 900 * LANE // cb, BF16_ROWS)
    col = lambda idx: pl.BlockSpec((None, Lp, cb), idx)
    wcol = lambda idx: pl.BlockSpec((3, cb), idx)
    return pl.pallas_call(
        functools.partial(_hyena_kernel, L=L, g=g, t=t),
        grid=(ncb, B, HYENA_ORDER),
        in_specs=[col(lambda c, b, o: (b, 0, o * ncb + c)), col(lambda c, b, o: (b, 0, HYENA_ORDER * ncb + c)),
                  wcol(lambda c, b, o: (0, o * ncb + c)), wcol(lambda c, b, o: (0, HYENA_ORDER * ncb + c)),
                  pl.BlockSpec((None, 1, cb), lambda c, b, o: (o, 0, c)),
                  pl.BlockSpec((None, g.k1, 2, LANE, cb), lambda c, b, o: (o, 0, 0, 0, c)),
                  _const_spec(fa.shape), _const_spec(fbf.shape), _const_spec(fbi.shape)],
        out_specs=col(lambda c, b, o: (b, 0, c)),
        out_shape=jax.ShapeDtypeStruct((B, Lp, HYENA_WIDTH), BF16),
        scratch_shapes=[_plane_scratch(g.zrows, cb), _plane_scratch(g.zrows, cb),
                        _plane_scratch(LANE * g.k1p, cb, jnp.uint32)],
        compiler_params=pltpu.CompilerParams(dimension_semantics=("arbitrary",) * 3, vmem_limit_bytes=VMEM_LIMIT),
        name="hyena",
    )(u, u, conv_w, conv_w, hy_bias, hspec, fa, fbf, fbi)


def _mlp_kernel(h_ref, oa_ref, oh_ref, ga_ref, gh_ref, wo_ref, gpost_ref, gpre_ref, w1_ref, w2_ref, gmlp_ref, o_ref):
    oa = _rms(oa_ref[...].astype(F32), ga_ref[...]).astype(BF16)
    oh = _rms(oh_ref[...].astype(F32), gh_ref[...]).astype(BF16)
    mix = jnp.dot(jnp.concatenate([oa, oh], axis=1), wo_ref[...], preferred_element_type=F32)
    h1 = h_ref[...] + _rms(mix, gpost_ref[...])
    m = jnp.dot(_rms(h1, gpre_ref[...]).astype(BF16), w1_ref[...], preferred_element_type=F32)
    m = jnp.square(jnp.maximum(m, 0.0)).astype(BF16)
    m = jnp.dot(m, w2_ref[...], preferred_element_type=F32)
    o_ref[...] = h1 + _rms(m, gmlp_ref[...])


def _mlp(x, oa, oh, ga, gh, wo, gpost, gpre, w1, w2, gmlp):
    B, S, _ = x.shape
    Lp = oa.shape[0] // B
    tm = _pick_tile(S, 512, BF16_ROWS)
    row = pl.BlockSpec((None, tm, D_MODEL), lambda b, i: (b, i, 0))
    shifted = lambda w: pl.BlockSpec((pl.Element(tm), pl.Element(w)),
                                     lambda b, i: (pl.multiple_of(b * Lp + N_META + i * tm, BF16_ROWS), 0))
    consts = (ga, gh, wo, gpost, gpre, w1, w2, gmlp)
    return pl.pallas_call(
        _mlp_kernel,
        grid=(B, S // tm),
        in_specs=[row, shifted(ATTN_WIDTH), shifted(HYENA_WIDTH)] + [_const_spec(a.shape) for a in consts],
        out_specs=row,
        out_shape=jax.ShapeDtypeStruct((B, S, D_MODEL), F32),
        compiler_params=pltpu.CompilerParams(dimension_semantics=("arbitrary",) * 2, vmem_limit_bytes=VMEM_LIMIT),
        name="mlp",
    )(x, oa, oh, *consts)


def _rot_cols(w):
    half = QK_ROPE // 2
    return jnp.concatenate([-w[:, half:], w[:, :half]], axis=1)


def _prep_weights(w_in, w_uq, w_ukv):
    s1, s2, s3 = Q_LORA, Q_LORA + KV_LORA, Q_LORA + KV_LORA + QK_ROPE
    kr = w_in[:, s2:s3]
    w1 = jnp.concatenate([w_in[:, :s2], kr, _rot_cols(kr), w_in[:, s3:]], axis=1).astype(BF16)
    qh = QK_NOPE + QK_ROPE
    cols = []
    for h in range(N_HEADS):
        rope = w_uq[:, h * qh + QK_NOPE:(h + 1) * qh]
        cols += [w_uq[:, h * qh:h * qh + QK_NOPE], rope, _rot_cols(rope)]
    wuq = jnp.concatenate(cols, axis=1).astype(BF16)
    kvh = QK_NOPE + V_HEAD
    kcols = [w_ukv[:, h * kvh:h * kvh + QK_NOPE] for h in range(N_HEADS)]
    vcols = [w_ukv[:, h * kvh + QK_NOPE:(h + 1) * kvh] for h in range(N_HEADS)]
    wukv = jnp.concatenate(kcols + vcols, axis=1).astype(BF16)
    return w1, wuq, wukv


def _trunk(x, meta_tokens, pre_mix_g, w1, q_norm_g, wuq, kv_norm_g, wukv, conv_w,
           f_w1, f_b1, f_freq, f_w2, f_b2, f_w3, f_decay, hy_bias, attn_out_g, hy_out_g,
           wo, post_mix_g, pre_mlp_g, wff1, wff2, post_mlp_g):
    B, S, _ = x.shape
    L = S + N_META
    g = _Geom(L)
    fa, fbf, fbi, ct, st, feats = _tables(L)
    fa, fbf, fbi = (jnp.asarray(a).astype(BF16) for a in (fa, fbf, fbi))
    q, kn, kr, v, u = _proj(x, meta_tokens.astype(x.dtype), pre_mix_g, w1, q_norm_g, wuq, kv_norm_g, wukv,
                            jnp.asarray(ct), jnp.asarray(st))
    o_attn = _attention(q, kn, kr, v, L)
    hf = _filters(jnp.asarray(feats), f_w1, f_b1, f_freq, f_w2, f_b2, f_w3, f_decay, L)
    hspec = _spectrum(hf, fa, fbf, g)
    o_hy = _hyena(u, conv_w, hy_bias, hspec, fa, fbf, fbi, L, g)
    R = B * g.Lp
    return _mlp(x, o_attn.reshape(R, ATTN_WIDTH), o_hy.reshape(R, HYENA_WIDTH),
                attn_out_g, hy_out_g, wo, post_mix_g, pre_mlp_g, wff1, wff2, post_mlp_g)


def kernel(x_prompt, x_sample, meta_tokens, pre_mix_g, w_in, q_norm_g, w_uq, kv_norm_g, w_ukv, conv_w, f_w1, f_b1, f_freq, f_w2, f_b2, f_w3, f_decay, hy_bias, attn_out_g, hy_out_g, w_o, post_mix_g, pre_mlp_g, w_ff1, w_ff2, post_mlp_g):
    assert pre_mix_g.shape[0] == 1, "single-layer trunk"
    w1, wuq, wukv = _prep_weights(w_in[0], w_uq[0], w_ukv[0])
    shared = (meta_tokens, pre_mix_g, w1, q_norm_g, wuq, kv_norm_g, wukv, conv_w[0],
              f_w1[0], f_b1, f_freq, f_w2[0], f_b2, f_w3[0], f_decay, hy_bias[0][:, None, :], attn_out_g, hy_out_g,
              w_o[0].astype(BF16), post_mix_g, pre_mlp_g, w_ff1[0].astype(BF16), w_ff2[0].astype(BF16), post_mlp_g)
    return (_trunk(x_prompt, *shared), _trunk(x_sample, *shared))
```

```python
import functools
import math

import numpy as np
import jax
import jax.numpy as jnp
from jax import lax
from jax.experimental import pallas as pl
from jax.experimental.pallas import tpu as pltpu

F32 = jnp.float32
BF16 = jnp.bfloat16

D_MODEL = 1024
N_META = 16
ATTN_WIDTH = 512
HYENA_WIDTH = 512
N_HEADS = 4
QK_NOPE = 128
QK_ROPE = 64
V_HEAD = 128
Q_LORA = 256
KV_LORA = 128
ROPE_BASE = 10000.0
FILTER_EMB = 33
FILTER_BANDS = 16
HYENA_ORDER = 2
D_FF = 4096
NORM_EPS = 1e-6

LANE = 128
SUBLANE = 8
BF16_ROWS = 16
VMEM_LIMIT = 56 * 1024 * 1024
HYENA_VMEM_BUDGET = 44 * 1024 * 1024
HYENA_MAX_LANES = 256
STAGE_A_UNROLL = 64
STAGE_B_UNROLL_CAP = 22
ATTN_KV_CHUNK = 2048
ATTN_VMEM_BUDGET = 46 * 1024 * 1024

Q_SCALE = (QK_NOPE + QK_ROPE) ** -0.5 * math.log2(math.e)


def _round_up(x, m):
    return -(-x // m) * m


def _pick_tile(n, cap, mult):
    best = None
    for d in range(mult, min(n, cap) + 1, mult):
        if n % d == 0:
            best = d
    return n if best is None else best


def _rms(x, g):
    return x * lax.rsqrt(jnp.mean(x * x, axis=-1, keepdims=True) + NORM_EPS) * g


def _const_spec(shape):
    nd = len(shape)
    return pl.BlockSpec(shape, lambda *_: (0,) * nd, pipeline_mode=pl.Buffered(1))


class _Geom:
    def __init__(self, L):
        self.L = L
        self.n1h = -(-L // LANE)
        n1 = -(-(2 * L - 1) // LANE)
        self.n1 = n1 + (n1 % 2)
        self.N = self.n1 * LANE
        self.k1 = self.n1 // 2 + 1
        self.k1p = _round_up(self.k1, SUBLANE)
        self.ka = 2 * self.k1p
        self.n1hp = _round_up(self.n1h, SUBLANE)
        self.Lp = self.n1h * LANE
        self.zrows = self.n1hp * LANE


@functools.lru_cache(maxsize=None)
def _tables(L):
    g = _Geom(L)
    n2 = np.arange(LANE)[:, None, None]
    k1 = np.arange(g.k1)[None, :, None]
    n1 = np.arange(g.n1hp)[None, None, :]
    n = n1 * LANE + n2
    ang = 2.0 * np.pi * ((k1 * n) % g.N).astype(np.float64) / g.N
    valid = (n1 < g.n1h)
    fa = np.zeros((LANE, g.ka, g.n1hp), np.float32)
    fa[:, 0 : 2 * g.k1 : 2, :] = np.cos(ang) * valid
    fa[:, 1 : 2 * g.k1 : 2, :] = -np.sin(ang) * valid
    kk = np.arange(LANE)
    th = 2.0 * np.pi * ((kk[:, None] * kk[None, :]) % LANE) / LANE
    c, s = np.cos(th), np.sin(th)
    inter = np.stack([kk, LANE + kk], axis=1).reshape(-1)
    fbf = np.block([[c, s], [-s, c]]).astype(np.float32)[:, inter]
    fbi = np.block([[c, -s], [s, c]]).astype(np.float32)[inter, :]
    inv = 1.0 / (ROPE_BASE ** (np.arange(0, QK_ROPE, 2, dtype=np.float64) / QK_ROPE))
    pos = np.arange(g.Lp, dtype=np.float64)[:, None] * inv[None, :]
    pos = np.concatenate([pos, pos], axis=-1)
    ct = np.zeros((g.Lp, LANE), np.float32)
    st = np.zeros((g.Lp, LANE), np.float32)
    ct[:, :QK_ROPE] = np.cos(pos)
    st[:, :QK_ROPE] = np.sin(pos)
    t = np.linspace(0.0, 1.0, L)[:, None]
    w = 2.0 * np.pi * np.arange(L, dtype=np.float64)[:, None] / L
    f = np.linspace(1e-4, FILTER_BANDS - 1, FILTER_BANDS)[None, :]
    feats = np.zeros((g.Lp, FILTER_EMB), np.float32)
    feats[:L] = np.concatenate([t, np.cos(f * w), -np.sin(f * w)], axis=-1)
    return fa, fbf, fbi, ct, st, feats


def _rope128(x, ct, st):
    return x * ct + pltpu.roll(x, QK_ROPE, 1) * st


def _proj_kernel(x_ref, halo_ref, meta_ref, g_ref, w1_ref, gq_ref, wuq_ref, gkv_ref, wukv_ref, ct_ref, st_ref,
                 q_ref, kn_ref, kr_ref, v_ref, u_ref, a_scr, *, tm, n, S):
    i = pl.program_id(1)
    bm = x_ref.shape[0]
    norm = lambda rows: _rms(rows, g_ref[...]).astype(BF16)
    zeros = lambda r: jnp.zeros((r, D_MODEL), BF16)

    @pl.when(i == 0)
    def _():
        take = min(tm - N_META, S)
        a_scr[0:N_META, :] = norm(meta_ref[...])
        a_scr[N_META:N_META + take, :] = norm(x_ref[0:take, :])
        if N_META + take < tm:
            a_scr[N_META + take:, :] = zeros(tm - N_META - take)

    if n > 2:
        @pl.when((i > 0) & (i < n - 1))
        def _():
            a_scr[0:N_META, :] = norm(halo_ref[...])
            a_scr[N_META:, :] = norm(x_ref[0:tm - N_META, :])

    if n > 1:
        @pl.when(i == n - 1)
        def _():
            skip = n * tm - S - N_META
            a_scr[0:bm - skip, :] = norm(x_ref[skip:, :])
            a_scr[bm - skip:, :] = zeros(tm - bm + skip)

    p = jnp.dot(a_scr[...], w1_ref[...], preferred_element_type=F32)
    u_ref[...] = p[:, 512:].astype(BF16)
    ct = ct_ref[...]
    st = st_ref[...]
    kr_ref[...] = _rope128(p[:, 384:512], ct, st).astype(BF16)
    cq = _rms(p[:, :Q_LORA], gq_ref[...]).astype(BF16)
    q = jnp.dot(cq, wuq_ref[...], preferred_element_type=F32)
    for h in range(N_HEADS):
        lo = h * 2 * LANE
        q_ref[:, lo:lo + LANE] = (q[:, lo:lo + LANE] * Q_SCALE).astype(BF16)
        q_ref[:, lo + LANE:lo + 2 * LANE] = (_rope128(q[:, lo + LANE:lo + 2 * LANE], ct, st) * Q_SCALE).astype(BF16)
    ckv = _rms(p[:, Q_LORA:Q_LORA + KV_LORA], gkv_ref[...]).astype(BF16)
    kv = jnp.dot(ckv, wukv_ref[...], preferred_element_type=F32)
    kn_ref[...] = kv[:, :ATTN_WIDTH].astype(BF16)
    v_ref[...] = kv[:, ATTN_WIDTH:].astype(BF16)


def _proj(x, meta, g_pre, w1, gq, wuq, gkv, wukv, ct, st):
    B, S, _ = x.shape
    Lp = ct.shape[0]
    tm = _pick_tile(Lp, 900, BF16_ROWS)
    n = Lp // tm
    bm = min(tm, S)
    assert n == 1 or (bm == tm and n * tm - S >= N_META)
    x_rows = pl.BlockSpec((pl.Element(bm), pl.Element(D_MODEL)),
                          lambda b, i: (pl.multiple_of(b * S + jnp.minimum(i * tm, S - bm), SUBLANE), 0))
    halo = pl.BlockSpec((pl.Element(N_META), pl.Element(D_MODEL)),
                        lambda b, i: (pl.multiple_of(b * S + jnp.maximum(i * tm - N_META, 0), SUBLANE), 0))
    row = lambda w: pl.BlockSpec((None, tm, w), lambda b, i: (b, i, 0))
    tab = pl.BlockSpec((tm, LANE), lambda b, i: (i, 0))
    outs = [jax.ShapeDtypeStruct((B, Lp, w), BF16) for w in (4 * 2 * LANE, ATTN_WIDTH, LANE, ATTN_WIDTH, 3 * HYENA_WIDTH)]
    x2 = x.reshape(B * S, D_MODEL)
    return pl.pallas_call(
        functools.partial(_proj_kernel, tm=tm, n=n, S=S),
        grid=(B, n),
        in_specs=[x_rows, halo, _const_spec(meta.shape), _const_spec(g_pre.shape), _const_spec(w1.shape),
                  _const_spec(gq.shape), _const_spec(wuq.shape), _const_spec(gkv.shape), _const_spec(wukv.shape), tab, tab],
        out_specs=[row(4 * 2 * LANE), row(ATTN_WIDTH), row(LANE), row(ATTN_WIDTH), row(3 * HYENA_WIDTH)],
        out_shape=outs,
        scratch_shapes=[pltpu.VMEM((tm, D_MODEL), BF16)],
        compiler_params=pltpu.CompilerParams(dimension_semantics=("arbitrary", "arbitrary"), vmem_limit_bytes=VMEM_LIMIT),
        name="proj",
    )(x2, x2, meta, g_pre, w1, gq, wuq, gkv, wukv, ct, st)


def _attn_kernel(q_ref, kn_ref, kr_ref, v_ref, o_ref, *, L, Lp, tk, hps):
    tq = q_ref.shape[0]
    n_full = (Lp - 1) // tk
    last = n_full * tk

    for h in range(hps):
        q = q_ref[:, h * 2 * LANE:(h + 1) * 2 * LANE]
        cols = slice(h * LANE, (h + 1) * LANE)

        def chunk(carry, start, size, masked, q=q, cols=cols):
            m, l, acc = carry
            k = jnp.concatenate([kn_ref[start:start + size, cols], kr_ref[start:start + size, :]], axis=1)
            s = lax.dot_general(q, k, (((1,), (1,)), ((), ())), preferred_element_type=F32)
            if masked:
                col = lax.broadcasted_iota(jnp.int32, (1, size), 1)
                s = jnp.where(col < L - start, s, -1e30)
            m_new = jnp.maximum(m, jnp.max(s, axis=1, keepdims=True))
            alpha = jnp.exp2(m - m_new)
            p = jnp.exp2(s - m_new)
            l = alpha * l + jnp.sum(p, axis=1, keepdims=True)
            acc = alpha * acc + jnp.dot(p.astype(BF16), v_ref[start:start + size, cols], preferred_element_type=F32)
            return m_new, l, acc

        carry = (jnp.full((tq, 1), -1e30, F32), jnp.zeros((tq, 1), F32), jnp.zeros((tq, V_HEAD), F32))
        for i in range(n_full):
            carry = chunk(carry, i * tk, tk, False)
        _, l, acc = chunk(carry, last, Lp - last, True)
        o_ref[:, cols] = (acc / l).astype(BF16)


def _attention(q, kn, kr, v, L):
    B, Lp, _ = q.shape
    tk = min(ATTN_KV_CHUNK, Lp)
    resident = lambda hps: 2 * Lp * (2 * hps + 1) * LANE * 2
    hps = next((h for h in (4, 2) if resident(h) <= ATTN_VMEM_BUDGET // 2), 1)
    tq = max(t for t in range(BF16_ROWS, Lp + 1, BF16_ROWS)
             if Lp % t == 0 and (t == BF16_ROWS or resident(hps) + 4 * t * tk * 4 <= ATTN_VMEM_BUDGET))
    full = lambda w, idx: pl.BlockSpec((None, Lp, w), idx)
    return pl.pallas_call(
        functools.partial(_attn_kernel, L=L, Lp=Lp, tk=tk, hps=hps),
        grid=(B, N_HEADS // hps, Lp // tq),
        in_specs=[pl.BlockSpec((None, tq, hps * 2 * LANE), lambda b, h, i: (b, i, h)),
                  full(hps * LANE, lambda b, h, i: (b, 0, h)), full(LANE, lambda b, h, i: (b, 0, 0)),
                  full(hps * LANE, lambda b, h, i: (b, 0, h))],
        out_specs=pl.BlockSpec((None, tq, hps * V_HEAD), lambda b, h, i: (b, i, h)),
        out_shape=jax.ShapeDtypeStruct((B, Lp, ATTN_WIDTH), BF16),
        compiler_params=pltpu.CompilerParams(dimension_semantics=("arbitrary",) * 3, vmem_limit_bytes=VMEM_LIMIT),
        name="attn",
    )(q, kn, kr, v)


def _filt_kernel(feat_ref, w1_ref, b1_ref, fr_ref, w2_ref, b2_ref, w3_ref, dec_ref, o_ref, *, L, tf):
    hp = lax.Precision.HIGHEST
    f = feat_ref[...]
    fr = fr_ref[...]
    h = jnp.sin(fr * (jnp.dot(f, w1_ref[...], precision=hp, preferred_element_type=F32) + b1_ref[...]))
    h = jnp.sin(fr * (jnp.dot(h, w2_ref[...], precision=hp, preferred_element_type=F32) + b2_ref[...]))
    h = jnp.dot(h, w3_ref[...], precision=hp, preferred_element_type=F32)
    h = h * jnp.exp(-f[:, 0:1] * jnp.abs(dec_ref[...]))
    row = pl.program_id(0) * tf + lax.broadcasted_iota(jnp.int32, (tf, 1), 0)
    col = lax.broadcasted_iota(jnp.int32, (1, h.shape[1]), 1)
    is_bwd = ((col // HYENA_WIDTH) % 2) == 1
    keep = (row < L) & jnp.logical_not(is_bwd & (row == 0))
    o_ref[...] = jnp.where(keep, h, 0.0)


def _filters(feats, f_w1, f_b1, f_freq, f_w2, f_b2, f_w3, f_decay, L):
    Lp = feats.shape[0]
    tf = _pick_tile(Lp, 640, SUBLANE)
    nf = f_w3.shape[1]
    args = (f_w1, f_b1, f_freq, f_w2, f_b2, f_w3, f_decay)
    return pl.pallas_call(
        functools.partial(_filt_kernel, L=L, tf=tf),
        grid=(Lp // tf,),
        in_specs=[pl.BlockSpec((tf, FILTER_EMB), lambda i: (i, 0))] + [_const_spec(a.shape) for a in args],
        out_specs=pl.BlockSpec((tf, nf), lambda i: (i, 0)),
        out_shape=jax.ShapeDtypeStruct((Lp, nf), F32),
        compiler_params=pltpu.CompilerParams(dimension_semantics=("arbitrary",), vmem_limit_bytes=VMEM_LIMIT),
        name="filt",
    )(feats, *args)


def _load_planes(buf, rows):
    return jnp.concatenate([buf[j, rows, :] for j in range(buf.shape[0])], axis=1)


def _store_planes(buf, rows, val):
    for j in range(buf.shape[0]):
        buf[j, rows, :] = val[:, j * LANE:(j + 1) * LANE]


def _pack_rows(x):
    return pltpu.bitcast(x.astype(BF16), jnp.uint32)


def _unpack_rows(x):
    return pltpu.bitcast(x, BF16)


def _stage_a_fwd(zbuf, fa_ref, sbuf, g):
    def body(n2, c):
        x = _load_planes(zbuf, pl.ds(n2, g.n1hp, stride=LANE)).astype(BF16)
        _store_planes(sbuf, pl.ds(pl.multiple_of(n2 * g.k1p, SUBLANE), g.k1p),
                      _pack_rows(jnp.dot(fa_ref[n2], x, preferred_element_type=F32)))
        return c

    lax.fori_loop(0, LANE, body, 0, unroll=STAGE_A_UNROLL)


def _stage_a_inv(sbuf, fa_ref, ybuf, g):
    def body(n2, c):
        s = _unpack_rows(_load_planes(sbuf, pl.ds(pl.multiple_of(n2 * g.k1p, SUBLANE), g.k1p)))
        y = lax.dot_general(fa_ref[n2], s, (((0,), (0,)), ((), ())), preferred_element_type=F32)
        _store_planes(ybuf, pl.ds(n2, g.n1hp, stride=LANE), y)
        return c

    lax.fori_loop(0, LANE, body, 0, unroll=STAGE_A_UNROLL)


def _load_k1(sbuf, k1, g):
    return _unpack_rows(_load_planes(sbuf, pl.ds(k1, LANE, stride=g.k1p)))


def _store_k1(sbuf, k1, g, b):
    _store_planes(sbuf, pl.ds(k1, LANE, stride=g.k1p), _pack_rows(b))


def _stage_b_unroll(g):
    return _pick_tile(g.k1, STAGE_B_UNROLL_CAP, 1)


def _plane_scratch(rows, cb, dtype=F32):
    return pltpu.VMEM((cb // LANE, rows, LANE), dtype)


def _hyena_cb(g, bytes_per_lane):
    fixed = LANE * g.ka * LANE * 2
    for cb in (HYENA_MAX_LANES, LANE):
        if fixed + cb * bytes_per_lane <= HYENA_VMEM_BUDGET:
            return cb
    return LANE


def _spec_kernel(hf_ref, fa_ref, fbf_ref, out_ref, zbuf, sbuf, acc, *, g):
    d = pl.program_id(2)
    _store_planes(zbuf, slice(0, g.Lp), hf_ref[...])
    if g.zrows > g.Lp:
        zbuf[:, g.Lp:, :] = jnp.zeros((zbuf.shape[0], g.zrows - g.Lp, LANE), F32)
    _stage_a_fwd(zbuf, fa_ref, sbuf, g)

    def run(backward):
        def body(k1, c):
            x = jnp.dot(fbf_ref[...], _load_k1(sbuf, k1, g), preferred_element_type=F32)
            scale = jnp.where((k1 == 0) | (k1 == g.n1 // 2), 1.0, 2.0) / g.N
            xr = x[:LANE] * scale
            xi = x[LANE:] * scale
            if backward:
                out_ref[k1, 0] = (acc[k1, 0] + xr).astype(BF16)
                out_ref[k1, 1] = (acc[k1, 1] - xi).astype(BF16)
            else:
                acc[k1, 0] = xr
                acc[k1, 1] = xi
            return c

        lax.fori_loop(0, g.k1, body, 0, unroll=_stage_b_unroll(g))

    pl.when(d == 0)(lambda: run(False))
    pl.when(d == 1)(lambda: run(True))


def _spectrum(hf, fa, fbf, g):
    cb = _hyena_cb(g, 4 * (g.zrows + LANE * g.k1p + g.k1 * 2 * LANE) + 2 * 4 * g.Lp + 2 * 2 * g.k1 * 2 * LANE)
    ncb = HYENA_WIDTH // cb
    return pl.pallas_call(
        functools.partial(_spec_kernel, g=g),
        grid=(HYENA_ORDER, ncb, 2),
        in_specs=[pl.BlockSpec((g.Lp, cb), lambda o, c, d: (0, (o * 2 + d) * ncb + c)),
                  _const_spec(fa.shape), _const_spec(fbf.shape)],
        out_specs=pl.BlockSpec((None, g.k1, 2, LANE, cb), lambda o, c, d: (o, 0, 0, 0, c)),
        out_shape=jax.ShapeDtypeStruct((HYENA_ORDER, g.k1, 2, LANE, HYENA_WIDTH), BF16),
        scratch_shapes=[_plane_scratch(g.zrows, cb), _plane_scratch(LANE * g.k1p, cb, jnp.uint32),
                        pltpu.VMEM((g.k1, 2, LANE, cb), F32)],
        compiler_params=pltpu.CompilerParams(dimension_semantics=("arbitrary",) * 3, vmem_limit_bytes=VMEM_LIMIT),
        name="spec",
    )(hf, fa, fbf)


def _short_conv(u_ref, w_ref, r0, t, Lp):
    halo = BF16_ROWS
    cb = u_ref.shape[1]
    parts = []
    if r0 == 0:
        parts.append(jnp.zeros((halo, cb), F32))
    lo = max(r0 - halo, 0)
    hi = min(r0 + t + halo, Lp)
    parts.append(u_ref[lo:hi, :].astype(F32))
    if r0 + t == Lp:
        parts.append(jnp.zeros((halo, cb), F32))
    x = jnp.concatenate(parts, axis=0) if len(parts) > 1 else parts[0]
    n = t + 2 * halo
    xm1 = pltpu.roll(x, 1, 0)[halo:halo + t]
    xp1 = pltpu.roll(x, n - 1, 0)[halo:halo + t]
    return xm1 * w_ref[0:1, :] + x[halo:halo + t] * w_ref[1:2, :] + xp1 * w_ref[2:3, :]


def _hyena_kernel(gate_ref, v_ref, wg_ref, wv_ref, bias_ref, h_ref, fa_ref, fbf_ref, fbi_ref,
                  out_ref, zbuf, ybuf, sbuf, *, L, g, t):
    order = pl.program_id(2)
    Lp = g.Lp
    nchunk = Lp // t

    def zero_pad_rows(val, r0):
        if r0 + t <= L:
            return val
        return jnp.where((r0 + lax.broadcasted_iota(jnp.int32, (t, 1), 0)) < L, val, 0.0)

    @pl.when(order == 0)
    def _():
        if g.zrows > Lp:
            zbuf[:, Lp:, :] = jnp.zeros((zbuf.shape[0], g.zrows - Lp, LANE), F32)
        for c in range(nchunk):
            r0 = c * t
            _store_planes(zbuf, slice(r0, r0 + t),
                          zero_pad_rows(_short_conv(v_ref, wv_ref, r0, t, Lp), r0))

    _stage_a_fwd(zbuf, fa_ref, sbuf, g)

    def body(k1, c):
        x = jnp.dot(fbf_ref[...], _load_k1(sbuf, k1, g), preferred_element_type=F32)
        xr, xi = x[:LANE], x[LANE:]
        hr = h_ref[k1, 0].astype(F32)
        hi = h_ref[k1, 1].astype(F32)
        y = jnp.concatenate([xr * hr - xi * hi, xr * hi + xi * hr], axis=0).astype(BF16)
        _store_k1(sbuf, k1, g, jnp.dot(fbi_ref[...], y, preferred_element_type=F32))
        return c

    lax.fori_loop(0, g.k1, body, 0, unroll=_stage_b_unroll(g))
    _stage_a_inv(sbuf, fa_ref, ybuf, g)

    bias = bias_ref[...]
    for c in range(nchunk):
        r0 = c * t
        gate = _short_conv(gate_ref, wg_ref, r0, t, Lp)
        rows = slice(r0, r0 + t)
        zn = zero_pad_rows(gate * (_load_planes(ybuf, rows) + bias * _load_planes(zbuf, rows)), r0)
        _store_planes(zbuf, rows, zn)
        out_ref[rows, :] = zn.astype(BF16)


def _hyena(u, conv_w, hy_bias, hspec, fa, fbf, fbi, L, g):
    B, Lp, _ = u.shape
    cb = _hyena_cb(g, 4 * (2 * g.zrows + LANE * g.k1p) + 2 * 2 * g.k1 * 2 * LANE + 6 * 2 * Lp)
    ncb = HYENA_WIDTH // cb
    t = _pick_tile(Lp, 900 * LANE // cb, BF16_ROWS)
    col = lambda idx: pl.BlockSpec((None, Lp, cb), idx)
    wcol = lambda idx: pl.BlockSpec((3, cb), idx)
    return pl.pallas_call(
        functools.partial(_hyena_kernel, L=L, g=g, t=t),
        grid=(ncb, B, HYENA_ORDER),
        in_specs=[col(lambda c, b, o: (b, 0, o * ncb + c)), col(lambda c, b, o: (b, 0, HYENA_ORDER * ncb + c)),
                  wcol(lambda c, b, o: (0, o * ncb + c)), wcol(lambda c, b, o: (0, HYENA_ORDER * ncb + c)),
                  pl.BlockSpec((None, 1, cb), lambda c, b, o: (o, 0, c)),
                  pl.BlockSpec((None, g.k1, 2, LANE, cb), lambda c, b, o: (o, 0, 0, 0, c)),
                  _const_spec(fa.shape), _const_spec(fbf.shape), _const_spec(fbi.shape)],
        out_specs=col(lambda c, b, o: (b, 0, c)),
        out_shape=jax.ShapeDtypeStruct((B, Lp, HYENA_WIDTH), BF16),
        scratch_shapes=[_plane_scratch(g.zrows, cb), _plane_scratch(g.zrows, cb),
                        _plane_scratch(LANE * g.k1p, cb, jnp.uint32)],
        compiler_params=pltpu.CompilerParams(dimension_semantics=("arbitrary",) * 3, vmem_limit_bytes=VMEM_LIMIT),
        name="hyena",
    )(u, u, conv_w, conv_w, hy_bias, hspec, fa, fbf, fbi)


def _mlp_kernel(h_ref, oa_ref, oh_ref, ga_ref, gh_ref, wo_ref, gpost_ref, gpre_ref, w1_ref, w2_ref, gmlp_ref, o_ref):
    oa = _rms(oa_ref[...].astype(F32), ga_ref[...]).astype(BF16)
    oh = _rms(oh_ref[...].astype(F32), gh_ref[...]).astype(BF16)
    mix = jnp.dot(jnp.concatenate([oa, oh], axis=1), wo_ref[...], preferred_element_type=F32)
    h1 = h_ref[...] + _rms(mix, gpost_ref[...])
    m = jnp.dot(_rms(h1, gpre_ref[...]).astype(BF16), w1_ref[...], preferred_element_type=F32)
    m = jnp.square(jnp.maximum(m, 0.0)).astype(BF16)
    m = jnp.dot(m, w2_ref[...], preferred_element_type=F32)
    o_ref[...] = h1 + _rms(m, gmlp_ref[...])


def _mlp(x, oa, oh, ga, gh, wo, gpost, gpre, w1, w2, gmlp):
    B, S, _ = x.shape
    Lp = oa.shape[0] // B
    tm = _pick_tile(S, 512, BF16_ROWS)
    row = pl.BlockSpec((None, tm, D_MODEL), lambda b, i: (b, i, 0))
    shifted = lambda w: pl.BlockSpec((pl.Element(tm), pl.Element(w)),
                                     lambda b, i: (pl.multiple_of(b * Lp + N_META + i * tm, BF16_ROWS), 0))
    consts = (ga, gh, wo, gpost, gpre, w1, w2, gmlp)
    return pl.pallas_call(
        _mlp_kernel,
        grid=(B, S // tm),
        in_specs=[row, shifted(ATTN_WIDTH), shifted(HYENA_WIDTH)] + [_const_spec(a.shape) for a in consts],
        out_specs=row,
        out_shape=jax.ShapeDtypeStruct((B, S, D_MODEL), F32),
        compiler_params=pltpu.CompilerParams(dimension_semantics=("arbitrary",) * 2, vmem_limit_bytes=VMEM_LIMIT),
        name="mlp",
    )(x, oa, oh, *consts)


def _rot_cols(w):
    half = QK_ROPE // 2
    return jnp.concatenate([-w[:, half:], w[:, :half]], axis=1)


def _prep_weights(w_in, w_uq, w_ukv):
    s1, s2, s3 = Q_LORA, Q_LORA + KV_LORA, Q_LORA + KV_LORA + QK_ROPE
    kr = w_in[:, s2:s3]
    w1 = jnp.concatenate([w_in[:, :s2], kr, _rot_cols(kr), w_in[:, s3:]], axis=1).astype(BF16)
    qh = QK_NOPE + QK_ROPE
    cols = []
    for h in range(N_HEADS):
        rope = w_uq[:, h * qh + QK_NOPE:(h + 1) * qh]
        cols += [w_uq[:, h * qh:h * qh + QK_NOPE], rope, _rot_cols(rope)]
    wuq = jnp.concatenate(cols, axis=1).astype(BF16)
    kvh = QK_NOPE + V_HEAD
    kcols = [w_ukv[:, h * kvh:h * kvh + QK_NOPE] for h in range(N_HEADS)]
    vcols = [w_ukv[:, h * kvh + QK_NOPE:(h + 1) * kvh] for h in range(N_HEADS)]
    wukv = jnp.concatenate(kcols + vcols, axis=1).astype(BF16)
    return w1, wuq, wukv


def _trunk(x, meta_tokens, pre_mix_g, w1, q_norm_g, wuq, kv_norm_g, wukv, conv_w,
           f_w1, f_b1, f_freq, f_w2, f_b2, f_w3, f_decay, hy_bias, attn_out_g, hy_out_g,
           wo, post_mix_g, pre_mlp_g, wff1, wff2, post_mlp_g):
    B, S, _ = x.shape
    L = S + N_META
    g = _Geom(L)
    fa, fbf, fbi, ct, st, feats = _tables(L)
    fa, fbf, fbi = (jnp.asarray(a).astype(BF16) for a in (fa, fbf, fbi))
    q, kn, kr, v, u = _proj(x, meta_tokens.astype(x.dtype), pre_mix_g, w1, q_norm_g, wuq, kv_norm_g, wukv,
                            jnp.asarray(ct), jnp.asarray(st))
    o_attn = _attention(q, kn, kr, v, L)
    hf = _filters(jnp.asarray(feats), f_w1, f_b1, f_freq, f_w2, f_b2, f_w3, f_decay, L)
    hspec = _spectrum(hf, fa, fbf, g)
    o_hy = _hyena(u, conv_w, hy_bias, hspec, fa, fbf, fbi, L, g)
    R = B * g.Lp
    return _mlp(x, o_attn.reshape(R, ATTN_WIDTH), o_hy.reshape(R, HYENA_WIDTH),
                attn_out_g, hy_out_g, wo, post_mix_g, pre_mlp_g, wff1, wff2, post_mlp_g)


def kernel(x_prompt, x_sample, meta_tokens, pre_mix_g, w_in, q_norm_g, w_uq, kv_norm_g, w_ukv, conv_w, f_w1, f_b1, f_freq, f_w2, f_b2, f_w3, f_decay, hy_bias, attn_out_g, hy_out_g, w_o, post_mix_g, pre_mlp_g, w_ff1, w_ff2, post_mlp_g):
    assert pre_mix_g.shape[0] == 1, "single-layer trunk"
    w1, wuq, wukv = _prep_weights(w_in[0], w_uq[0], w_ukv[0])
    shared = (meta_tokens, pre_mix_g, w1, q_norm_g, wuq, kv_norm_g, wukv, conv_w[0],
              f_w1[0], f_b1, f_freq, f_w2[0], f_b2, f_w3[0], f_decay, hy_bias[0][:, None, :], attn_out_g, hy_out_g,
              w_o[0].astype(BF16), post_mix_g, pre_mlp_g, w_ff1[0].astype(BF16), w_ff2[0].astype(BF16), post_mlp_g)
    return (_trunk(x_prompt, *shared), _trunk(x_sample, *shared))
```

```python
import functools
import math

import numpy as np
import jax
import jax.numpy as jnp
from jax import lax
from jax.experimental import pallas as pl
from jax.experimental.pallas import tpu as pltpu

F32 = jnp.float32
BF16 = jnp.bfloat16

D_MODEL = 1024
N_META = 16
ATTN_WIDTH = 512
HYENA_WIDTH = 512
N_HEADS = 4
QK_NOPE = 128
QK_ROPE = 64
V_HEAD = 128
Q_LORA = 256
KV_LORA = 128
ROPE_BASE = 10000.0
FILTER_EMB = 33
FILTER_BANDS = 16
HYENA_ORDER = 2
D_FF = 4096
NORM_EPS = 1e-6

LANE = 128
SUBLANE = 8
BF16_ROWS = 16
VMEM_LIMIT = 56 * 1024 * 1024
HYENA_VMEM_BUDGET = 44 * 1024 * 1024
HYENA_MAX_LANES = 256
STAGE_A_UNROLL = LANE
STAGE_B_UNROLL_CAP = 33
ATTN_KV_CHUNK = 2048
ATTN_VMEM_BUDGET = 46 * 1024 * 1024

Q_SCALE = (QK_NOPE + QK_ROPE) ** -0.5 * math.log2(math.e)


def _round_up(x, m):
    return -(-x // m) * m


def _pick_tile(n, cap, mult):
    best = None
    for d in range(mult, min(n, cap) + 1, mult):
        if n % d == 0:
            best = d
    return n if best is None else best


def _rms(x, g):
    return x * lax.rsqrt(jnp.mean(x * x, axis=-1, keepdims=True) + NORM_EPS) * g


def _const_spec(shape):
    nd = len(shape)
    return pl.BlockSpec(shape, lambda *_: (0,) * nd, pipeline_mode=pl.Buffered(1))


class _Geom:
    def __init__(self, L):
        self.L = L
        self.n1h = -(-L // LANE)
        n1 = -(-(2 * L - 1) // LANE)
        self.n1 = n1 + (n1 % 2)
        self.N = self.n1 * LANE
        self.k1 = self.n1 // 2 + 1
        self.k1p = _round_up(self.k1, SUBLANE)
        self.ka = 2 * self.k1p
        self.n1hp = _round_up(self.n1h, SUBLANE)
        self.Lp = self.n1h * LANE
        self.zrows = self.n1hp * LANE


@functools.lru_cache(maxsize=None)
def _tables(L):
    g = _Geom(L)
    n2 = np.arange(LANE)[:, None, None]
    k1 = np.arange(g.k1)[None, :, None]
    n1 = np.arange(g.n1hp)[None, None, :]
    n = n1 * LANE + n2
    ang = 2.0 * np.pi * ((k1 * n) % g.N).astype(np.float64) / g.N
    valid = (n1 < g.n1h)
    fa = np.zeros((LANE, g.ka, g.n1hp), np.float32)
    fa[:, 0 : 2 * g.k1 : 2, :] = np.cos(ang) * valid
    fa[:, 1 : 2 * g.k1 : 2, :] = -np.sin(ang) * valid
    kk = np.arange(LANE)
    th = 2.0 * np.pi * ((kk[:, None] * kk[None, :]) % LANE) / LANE
    c, s = np.cos(th), np.sin(th)
    inter = np.stack([kk, LANE + kk], axis=1).reshape(-1)
    fbf = np.block([[c, s], [-s, c]]).astype(np.float32)[:, inter]
    fbi = np.block([[c, -s], [s, c]]).astype(np.float32)[inter, :]
    inv = 1.0 / (ROPE_BASE ** (np.arange(0, QK_ROPE, 2, dtype=np.float64) / QK_ROPE))
    pos = np.arange(g.Lp, dtype=np.float64)[:, None] * inv[None, :]
    pos = np.concatenate([pos, pos], axis=-1)
    ct = np.zeros((g.Lp, LANE), np.float32)
    st = np.zeros((g.Lp, LANE), np.float32)
    ct[:, :QK_ROPE] = np.cos(pos)
    st[:, :QK_ROPE] = np.sin(pos)
    t = np.linspace(0.0, 1.0, L)[:, None]
    w = 2.0 * np.pi * np.arange(L, dtype=np.float64)[:, None] / L
    f = np.linspace(1e-4, FILTER_BANDS - 1, FILTER_BANDS)[None, :]
    feats = np.zeros((g.Lp, FILTER_EMB), np.float32)
    feats[:L] = np.concatenate([t, np.cos(f * w), -np.sin(f * w)], axis=-1)
    return fa, fbf, fbi, ct, st, feats


def _rope128(x, ct, st):
    return x * ct + pltpu.roll(x, QK_ROPE, 1) * st


def _proj_kernel(x_ref, halo_ref, meta_ref, g_ref, w1_ref, gq_ref, wuq_ref, gkv_ref, wukv_ref, ct_ref, st_ref,
                 q_ref, kn_ref, kr_ref, v_ref, u_ref, a_scr, *, tm, n, S):
    i = pl.program_id(1)
    bm = x_ref.shape[0]
    norm = lambda rows: _rms(rows, g_ref[...]).astype(BF16)
    zeros = lambda r: jnp.zeros((r, D_MODEL), BF16)

    @pl.when(i == 0)
    def _():
        take = min(tm - N_META, S)
        a_scr[0:N_META, :] = norm(meta_ref[...])
        a_scr[N_META:N_META + take, :] = norm(x_ref[0:take, :])
        if N_META + take < tm:
            a_scr[N_META + take:, :] = zeros(tm - N_META - take)

    if n > 2:
        @pl.when((i > 0) & (i < n - 1))
        def _():
            a_scr[0:N_META, :] = norm(halo_ref[...])
            a_scr[N_META:, :] = norm(x_ref[0:tm - N_META, :])

    if n > 1:
        @pl.when(i == n - 1)
        def _():
            skip = n * tm - S - N_META
            a_scr[0:bm - skip, :] = norm(x_ref[skip:, :])
            a_scr[bm - skip:, :] = zeros(tm - bm + skip)

    p = jnp.dot(a_scr[...], w1_ref[...], preferred_element_type=F32)
    u_ref[...] = p[:, 512:].astype(BF16)
    ct = ct_ref[...]
    st = st_ref[...]
    kr_ref[...] = _rope128(p[:, 384:512], ct, st).astype(BF16)
    cq = _rms(p[:, :Q_LORA], gq_ref[...]).astype(BF16)
    q = jnp.dot(cq, wuq_ref[...], preferred_element_type=F32)
    for h in range(N_HEADS):
        lo = h * 2 * LANE
        q_ref[:, lo:lo + LANE] = (q[:, lo:lo + LANE] * Q_SCALE).astype(BF16)
        q_ref[:, lo + LANE:lo + 2 * LANE] = (_rope128(q[:, lo + LANE:lo + 2 * LANE], ct, st) * Q_SCALE).astype(BF16)
    ckv = _rms(p[:, Q_LORA:Q_LORA + KV_LORA], gkv_ref[...]).astype(BF16)
    kv = jnp.dot(ckv, wukv_ref[...], preferred_element_type=F32)
    kn_ref[...] = kv[:, :ATTN_WIDTH].astype(BF16)
    v_ref[...] = kv[:, ATTN_WIDTH:].astype(BF16)


def _proj(x, meta, g_pre, w1, gq, wuq, gkv, wukv, ct, st):
    B, S, _ = x.shape
    Lp = ct.shape[0]
    tm = _pick_tile(Lp, 900, BF16_ROWS)
    n = Lp // tm
    bm = min(tm, S)
    assert n == 1 or (bm == tm and n * tm - S >= N_META)
    x_rows = pl.BlockSpec((pl.Element(bm), pl.Element(D_MODEL)),
                          lambda b, i: (pl.multiple_of(b * S + jnp.minimum(i * tm, S - bm), SUBLANE), 0))
    halo = pl.BlockSpec((pl.Element(N_META), pl.Element(D_MODEL)),
                        lambda b, i: (pl.multiple_of(b * S + jnp.maximum(i * tm - N_META, 0), SUBLANE), 0))
    row = lambda w: pl.BlockSpec((None, tm, w), lambda b, i: (b, i, 0))
    tab = pl.BlockSpec((tm, LANE), lambda b, i: (i, 0))
    outs = [jax.ShapeDtypeStruct((B, Lp, w), BF16) for w in (4 * 2 * LANE, ATTN_WIDTH, LANE, ATTN_WIDTH, 3 * HYENA_WIDTH)]
    x2 = x.reshape(B * S, D_MODEL)
    return pl.pallas_call(
        functools.partial(_proj_kernel, tm=tm, n=n, S=S),
        grid=(B, n),
        in_specs=[x_rows, halo, _const_spec(meta.shape), _const_spec(g_pre.shape), _const_spec(w1.shape),
                  _const_spec(gq.shape), _const_spec(wuq.shape), _const_spec(gkv.shape), _const_spec(wukv.shape), tab, tab],
        out_specs=[row(4 * 2 * LANE), row(ATTN_WIDTH), row(LANE), row(ATTN_WIDTH), row(3 * HYENA_WIDTH)],
        out_shape=outs,
        scratch_shapes=[pltpu.VMEM((tm, D_MODEL), BF16)],
        compiler_params=pltpu.CompilerParams(dimension_semantics=("arbitrary", "arbitrary"), vmem_limit_bytes=VMEM_LIMIT),
        name="proj",
    )(x2, x2, meta, g_pre, w1, gq, wuq, gkv, wukv, ct, st)


def _attn_kernel(q_ref, kn_ref, kr_ref, v_ref, o_ref, *, L, Lp, tk, hps):
    tq = q_ref.shape[0]
    n_full = (Lp - 1) // tk
    last = n_full * tk

    for h in range(hps):
        q = q_ref[:, h * 2 * LANE:(h + 1) * 2 * LANE]
        cols = slice(h * LANE, (h + 1) * LANE)

        def chunk(carry, start, size, masked, q=q, cols=cols):
            m, l, acc = carry
            k = jnp.concatenate([kn_ref[start:start + size, cols], kr_ref[start:start + size, :]], axis=1)
            s = lax.dot_general(q, k, (((1,), (1,)), ((), ())), preferred_element_type=F32)
            if masked:
                col = lax.broadcasted_iota(jnp.int32, (1, size), 1)
                s = jnp.where(col < L - start, s, -1e30)
            m_new = jnp.maximum(m, jnp.max(s, axis=1, keepdims=True))
            alpha = jnp.exp2(m - m_new)
            p = jnp.exp2(s - m_new)
            l = alpha * l + jnp.sum(p, axis=1, keepdims=True)
            acc = alpha * acc + jnp.dot(p.astype(BF16), v_ref[start:start + size, cols], preferred_element_type=F32)
            return m_new, l, acc

        carry = (jnp.full((tq, 1), -1e30, F32), jnp.zeros((tq, 1), F32), jnp.zeros((tq, V_HEAD), F32))
        for i in range(n_full):
            carry = chunk(carry, i * tk, tk, False)
        _, l, acc = chunk(carry, last, Lp - last, True)
        o_ref[:, cols] = (acc / l).astype(BF16)


def _attention(q, kn, kr, v, L):
    B, Lp, _ = q.shape
    tk = min(ATTN_KV_CHUNK, Lp)
    resident = lambda hps: 2 * Lp * (2 * hps + 1) * LANE * 2
    hps = next((h for h in (4, 2) if resident(h) <= ATTN_VMEM_BUDGET // 2), 1)
    tq = max(t for t in range(BF16_ROWS, Lp + 1, BF16_ROWS)
             if Lp % t == 0 and (t == BF16_ROWS or resident(hps) + 4 * t * tk * 4 <= ATTN_VMEM_BUDGET))
    full = lambda w, idx: pl.BlockSpec((None, Lp, w), idx)
    return pl.pallas_call(
        functools.partial(_attn_kernel, L=L, Lp=Lp, tk=tk, hps=hps),
        grid=(B, N_HEADS // hps, Lp // tq),
        in_specs=[pl.BlockSpec((None, tq, hps * 2 * LANE), lambda b, h, i: (b, i, h)),
                  full(hps * LANE, lambda b, h, i: (b, 0, h)), full(LANE, lambda b, h, i: (b, 0, 0)),
                  full(hps * LANE, lambda b, h, i: (b, 0, h))],
        out_specs=pl.BlockSpec((None, tq, hps * V_HEAD), lambda b, h, i: (b, i, h)),
        out_shape=jax.ShapeDtypeStruct((B, Lp, ATTN_WIDTH), BF16),
        compiler_params=pltpu.CompilerParams(dimension_semantics=("arbitrary",) * 3, vmem_limit_bytes=VMEM_LIMIT),
        name="attn",
    )(q, kn, kr, v)


def _filt_kernel(feat_ref, w1_ref, b1_ref, fr_ref, w2_ref, b2_ref, w3_ref, dec_ref, o_ref, *, L, tf):
    hp = lax.Precision.HIGHEST
    f = feat_ref[...]
    fr = fr_ref[...]
    h = jnp.sin(fr * (jnp.dot(f, w1_ref[...], precision=hp, preferred_element_type=F32) + b1_ref[...]))
    h = jnp.sin(fr * (jnp.dot(h, w2_ref[...], precision=hp, preferred_element_type=F32) + b2_ref[...]))
    h = jnp.dot(h, w3_ref[...], precision=hp, preferred_element_type=F32)
    h = h * jnp.exp(-f[:, 0:1] * jnp.abs(dec_ref[...]))
    row = pl.program_id(0) * tf + lax.broadcasted_iota(jnp.int32, (tf, 1), 0)
    col = lax.broadcasted_iota(jnp.int32, (1, h.shape[1]), 1)
    is_bwd = ((col // HYENA_WIDTH) % 2) == 1
    keep = (row < L) & jnp.logical_not(is_bwd & (row == 0))
    o_ref[...] = jnp.where(keep, h, 0.0)


def _filters(feats, f_w1, f_b1, f_freq, f_w2, f_b2, f_w3, f_decay, L):
    Lp = feats.shape[0]
    tf = _pick_tile(Lp, 640, SUBLANE)
    nf = f_w3.shape[1]
    args = (f_w1, f_b1, f_freq, f_w2, f_b2, f_w3, f_decay)
    return pl.pallas_call(
        functools.partial(_filt_kernel, L=L, tf=tf),
        grid=(Lp // tf,),
        in_specs=[pl.BlockSpec((tf, FILTER_EMB), lambda i: (i, 0))] + [_const_spec(a.shape) for a in args],
        out_specs=pl.BlockSpec((tf, nf), lambda i: (i, 0)),
        out_shape=jax.ShapeDtypeStruct((Lp, nf), F32),
        compiler_params=pltpu.CompilerParams(dimension_semantics=("arbitrary",), vmem_limit_bytes=VMEM_LIMIT),
        name="filt",
    )(feats, *args)


def _load_planes(buf, rows):
    return jnp.concatenate([buf[j, rows, :] for j in range(buf.shape[0])], axis=1)


def _store_planes(buf, rows, val):
    for j in range(buf.shape[0]):
        buf[j, rows, :] = val[:, j * LANE:(j + 1) * LANE]


def _pack_rows(x):
    return pltpu.bitcast(x.astype(BF16), jnp.uint32)


def _unpack_rows(x):
    return pltpu.bitcast(x, BF16)


def _stage_a_fwd(zbuf, fa_ref, sbuf, g):
    def body(n2, c):
        x = _load_planes(zbuf, pl.ds(n2, g.n1hp, stride=LANE)).astype(BF16)
        _store_planes(sbuf, pl.ds(pl.multiple_of(n2 * g.k1p, SUBLANE), g.k1p),
                      _pack_rows(jnp.dot(fa_ref[n2], x, preferred_element_type=F32)))
        return c

    lax.fori_loop(0, LANE, body, 0, unroll=STAGE_A_UNROLL)


def _stage_a_inv(sbuf, fa_ref, ybuf, g):
    def body(n2, c):
        s = _unpack_rows(_load_planes(sbuf, pl.ds(pl.multiple_of(n2 * g.k1p, SUBLANE), g.k1p)))
        y = lax.dot_general(fa_ref[n2], s, (((0,), (0,)), ((), ())), preferred_element_type=F32)
        _store_planes(ybuf, pl.ds(n2, g.n1hp, stride=LANE), y)
        return c

    lax.fori_loop(0, LANE, body, 0, unroll=STAGE_A_UNROLL)


def _load_k1(sbuf, k1, g):
    return _unpack_rows(_load_planes(sbuf, pl.ds(k1, LANE, stride=g.k1p)))


def _store_k1(sbuf, k1, g, b):
    _store_planes(sbuf, pl.ds(k1, LANE, stride=g.k1p), _pack_rows(b))


def _stage_b_unroll(g):
    return _pick_tile(g.k1, STAGE_B_UNROLL_CAP, 1)


def _plane_scratch(rows, cb, dtype=F32):
    return pltpu.VMEM((cb // LANE, rows, LANE), dtype)


def _hyena_cb(g, bytes_per_lane):
    fixed = LANE * g.ka * LANE * 2
    for cb in (HYENA_MAX_LANES, LANE):
        if fixed + cb * bytes_per_lane <= HYENA_VMEM_BUDGET:
            return cb
    return LANE


def _spec_kernel(hf_ref, fa_ref, fbf_ref, out_ref, zbuf, sbuf, acc, *, g):
    d = pl.program_id(2)
    _store_planes(zbuf, slice(0, g.Lp), hf_ref[...])
    if g.zrows > g.Lp:
        zbuf[:, g.Lp:, :] = jnp.zeros((zbuf.shape[0], g.zrows - g.Lp, LANE), F32)
    _stage_a_fwd(zbuf, fa_ref, sbuf, g)

    def run(backward):
        def body(k1, c):
            x = jnp.dot(fbf_ref[...], _load_k1(sbuf, k1, g), preferred_element_type=F32)
            scale = jnp.where((k1 == 0) | (k1 == g.n1 // 2), 1.0, 2.0) / g.N
            xr = x[:LANE] * scale
            xi = x[LANE:] * scale
            if backward:
                out_ref[k1, 0] = (acc[k1, 0] + xr).astype(BF16)
                out_ref[k1, 1] = (acc[k1, 1] - xi).astype(BF16)
            else:
                acc[k1, 0] = xr
                acc[k1, 1] = xi
            return c

        lax.fori_loop(0, g.k1, body, 0, unroll=_stage_b_unroll(g))

    pl.when(d == 0)(lambda: run(False))
    pl.when(d == 1)(lambda: run(True))


def _spectrum(hf, fa, fbf, g):
    cb = _hyena_cb(g, 4 * (g.zrows + LANE * g.k1p + g.k1 * 2 * LANE) + 2 * 4 * g.Lp + 2 * 2 * g.k1 * 2 * LANE)
    ncb = HYENA_WIDTH // cb
    return pl.pallas_call(
        functools.partial(_spec_kernel, g=g),
        grid=(HYENA_ORDER, ncb, 2),
        in_specs=[pl.BlockSpec((g.Lp, cb), lambda o, c, d: (0, (o * 2 + d) * ncb + c)),
                  _const_spec(fa.shape), _const_spec(fbf.shape)],
        out_specs=pl.BlockSpec((None, g.k1, 2, LANE, cb), lambda o, c, d: (o, 0, 0, 0, c)),
        out_shape=jax.ShapeDtypeStruct((HYENA_ORDER, g.k1, 2, LANE, HYENA_WIDTH), BF16),
        scratch_shapes=[_plane_scratch(g.zrows, cb), _plane_scratch(LANE * g.k1p, cb, jnp.uint32),
                        pltpu.VMEM((g.k1, 2, LANE, cb), F32)],
        compiler_params=pltpu.CompilerParams(dimension_semantics=("arbitrary",) * 3, vmem_limit_bytes=VMEM_LIMIT),
        name="spec",
    )(hf, fa, fbf)


def _short_conv(u_ref, w_ref, r0, t, Lp):
    halo = BF16_ROWS
    cb = u_ref.shape[1]
    parts = []
    if r0 == 0:
        parts.append(jnp.zeros((halo, cb), F32))
    lo = max(r0 - halo, 0)
    hi = min(r0 + t + halo, Lp)
    parts.append(u_ref[lo:hi, :].astype(F32))
    if r0 + t == Lp:
        parts.append(jnp.zeros((halo, cb), F32))
    x = jnp.concatenate(parts, axis=0) if len(parts) > 1 else parts[0]
    n = t + 2 * halo
    xm1 = pltpu.roll(x, 1, 0)[halo:halo + t]
    xp1 = pltpu.roll(x, n - 1, 0)[halo:halo + t]
    return xm1 * w_ref[0:1, :] + x[halo:halo + t] * w_ref[1:2, :] + xp1 * w_ref[2:3, :]


def _hyena_kernel(gate_ref, v_ref, wg_ref, wv_ref, bias_ref, h_ref, fa_ref, fbf_ref, fbi_ref,
                  out_ref, zbuf, ybuf, sbuf, *, L, g, t):
    order = pl.program_id(2)
    Lp = g.Lp
    nchunk = Lp // t

    def zero_pad_rows(val, r0):
        if r0 + t <= L:
            return val
        return jnp.where((r0 + lax.broadcasted_iota(jnp.int32, (t, 1), 0)) < L, val, 0.0)

    @pl.when(order == 0)
    def _():
        if g.zrows > Lp:
            zbuf[:, Lp:, :] = jnp.zeros((zbuf.shape[0], g.zrows - Lp, LANE), F32)
        for c in range(nchunk):
            r0 = c * t
            _store_planes(zbuf, slice(r0, r0 + t),
                          zero_pad_rows(_short_conv(v_ref, wv_ref, r0, t, Lp), r0))

    _stage_a_fwd(zbuf, fa_ref, sbuf, g)

    def body(k1, c):
        x = jnp.dot(fbf_ref[...], _load_k1(sbuf, k1, g), preferred_element_type=F32)
        xr, xi = x[:LANE], x[LANE:]
        hr = h_ref[k1, 0].astype(F32)
        hi = h_ref[k1, 1].astype(F32)
        y = jnp.concatenate([xr * hr - xi * hi, xr * hi + xi * hr], axis=0).astype(BF16)
        _store_k1(sbuf, k1, g, jnp.dot(fbi_ref[...], y, preferred_element_type=F32))
        return c

    lax.fori_loop(0, g.k1, body, 0, unroll=_stage_b_unroll(g))
    _stage_a_inv(sbuf, fa_ref, ybuf, g)

    bias = bias_ref[...]
    for c in range(nchunk):
        r0 = c * t
        gate = _short_conv(gate_ref, wg_ref, r0, t, Lp)
        rows = slice(r0, r0 + t)
        zn = zero_pad_rows(gate * (_load_planes(ybuf, rows) + bias * _load_planes(zbuf, rows)), r0)
        _store_planes(zbuf, rows, zn)
        out_ref[rows, :] = zn.astype(BF16)


def _hyena(u, conv_w, hy_bias, hspec, fa, fbf, fbi, L, g):
    B, Lp, _ = u.shape
    cb = _hyena_cb(g, 4 * (2 * g.zrows + LANE * g.k1p) + 2 * 2 * g.k1 * 2 * LANE + 6 * 2 * Lp)
    ncb = HYENA_WIDTH // cb
    t = _pick_tile(Lp, 900 * LANE // cb, BF16_ROWS)
    col = lambda idx: pl.BlockSpec((None, Lp, cb), idx)
    wcol = lambda idx: pl.BlockSpec((3, cb), idx)
    return pl.pallas_call(
        functools.partial(_hyena_kernel, L=L, g=g, t=t),
        grid=(ncb, B, HYENA_ORDER),
        in_specs=[col(lambda c, b, o: (b, 0, o * ncb + c)), col(lambda c, b, o: (b, 0, HYENA_ORDER * ncb + c)),
                  wcol(lambda c, b, o: (0, o * ncb + c)), wcol(lambda c, b, o: (0, HYENA_ORDER * ncb + c)),
                  pl.BlockSpec((None, 1, cb), lambda c, b, o: (o, 0, c)),
                  pl.BlockSpec((None, g.k1, 2, LANE, cb), lambda c, b, o: (o, 0, 0, 0, c)),
                  _const_spec(fa.shape), _const_spec(fbf.shape), _const_spec(fbi.shape)],
        out_specs=col(lambda c, b, o: (b, 0, c)),
        out_shape=jax.ShapeDtypeStruct((B, Lp, HYENA_WIDTH), BF16),
        scratch_shapes=[_plane_scratch(g.zrows, cb), _plane_scratch(g.zrows, cb),
                        _plane_scratch(LANE * g.k1p, cb, jnp.uint32)],
        compiler_params=pltpu.CompilerParams(dimension_semantics=("arbitrary",) * 3, vmem_limit_bytes=VMEM_LIMIT),
        name="hyena",
    )(u, u, conv_w, conv_w, hy_bias, hspec, fa, fbf, fbi)


def _mlp_kernel(h_ref, oa_ref, oh_ref, ga_ref, gh_ref, wo_ref, gpost_ref, gpre_ref, w1_ref, w2_ref, gmlp_ref, o_ref):
    oa = _rms(oa_ref[...].astype(F32), ga_ref[...]).astype(BF16)
    oh = _rms(oh_ref[...].astype(F32), gh_ref[...]).astype(BF16)
    mix = jnp.dot(jnp.concatenate([oa, oh], axis=1), wo_ref[...], preferred_element_type=F32)
    h1 = h_ref[...] + _rms(mix, gpost_ref[...])
    m = jnp.dot(_rms(h1, gpre_ref[...]).astype(BF16), w1_ref[...], preferred_element_type=F32)
    m = jnp.square(jnp.maximum(m, 0.0)).astype(BF16)
    m = jnp.dot(m, w2_ref[...], preferred_element_type=F32)
    o_ref[...] = h1 + _rms(m, gmlp_ref[...])


def _mlp(x, oa, oh, ga, gh, wo, gpost, gpre, w1, w2, gmlp):
    B, S, _ = x.shape
    Lp = oa.shape[0] // B
    tm = _pick_tile(S, 512, BF16_ROWS)
    row = pl.BlockSpec((None, tm, D_MODEL), lambda b, i: (b, i, 0))
    shifted = lambda w: pl.BlockSpec((pl.Element(tm), pl.Element(w)),
                                     lambda b, i: (pl.multiple_of(b * Lp + N_META + i * tm, BF16_ROWS), 0))
    consts = (ga, gh, wo, gpost, gpre, w1, w2, gmlp)
    return pl.pallas_call(
        _mlp_kernel,
        grid=(B, S // tm),
        in_specs=[row, shifted(ATTN_WIDTH), shifted(HYENA_WIDTH)] + [_const_spec(a.shape) for a in consts],
        out_specs=row,
        out_shape=jax.ShapeDtypeStruct((B, S, D_MODEL), F32),
        compiler_params=pltpu.CompilerParams(dimension_semantics=("arbitrary",) * 2, vmem_limit_bytes=VMEM_LIMIT),
        name="mlp",
    )(x, oa, oh, *consts)


def _rot_cols(w):
    half = QK_ROPE // 2
    return jnp.concatenate([-w[:, half:], w[:, :half]], axis=1)


def _prep_weights(w_in, w_uq, w_ukv):
    s1, s2, s3 = Q_LORA, Q_LORA + KV_LORA, Q_LORA + KV_LORA + QK_ROPE
    kr = w_in[:, s2:s3]
    w1 = jnp.concatenate([w_in[:, :s2], kr, _rot_cols(kr), w_in[:, s3:]], axis=1).astype(BF16)
    qh = QK_NOPE + QK_ROPE
    cols = []
    for h in range(N_HEADS):
        rope = w_uq[:, h * qh + QK_NOPE:(h + 1) * qh]
        cols += [w_uq[:, h * qh:h * qh + QK_NOPE], rope, _rot_cols(rope)]
    wuq = jnp.concatenate(cols, axis=1).astype(BF16)
    kvh = QK_NOPE + V_HEAD
    kcols = [w_ukv[:, h * kvh:h * kvh + QK_NOPE] for h in range(N_HEADS)]
    vcols = [w_ukv[:, h * kvh + QK_NOPE:(h + 1) * kvh] for h in range(N_HEADS)]
    wukv = jnp.concatenate(kcols + vcols, axis=1).astype(BF16)
    return w1, wuq, wukv


def _trunk(x, meta_tokens, pre_mix_g, w1, q_norm_g, wuq, kv_norm_g, wukv, conv_w,
           f_w1, f_b1, f_freq, f_w2, f_b2, f_w3, f_decay, hy_bias, attn_out_g, hy_out_g,
           wo, post_mix_g, pre_mlp_g, wff1, wff2, post_mlp_g):
    B, S, _ = x.shape
    L = S + N_META
    g = _Geom(L)
    fa, fbf, fbi, ct, st, feats = _tables(L)
    fa, fbf, fbi = (jnp.asarray(a).astype(BF16) for a in (fa, fbf, fbi))
    q, kn, kr, v, u = _proj(x, meta_tokens.astype(x.dtype), pre_mix_g, w1, q_norm_g, wuq, kv_norm_g, wukv,
                            jnp.asarray(ct), jnp.asarray(st))
    o_attn = _attention(q, kn, kr, v, L)
    hf = _filters(jnp.asarray(feats), f_w1, f_b1, f_freq, f_w2, f_b2, f_w3, f_decay, L)
    hspec = _spectrum(hf, fa, fbf, g)
    o_hy = _hyena(u, conv_w, hy_bias, hspec, fa, fbf, fbi, L, g)
    R = B * g.Lp
    return _mlp(x, o_attn.reshape(R, ATTN_WIDTH), o_hy.reshape(R, HYENA_WIDTH),
                attn_out_g, hy_out_g, wo, post_mix_g, pre_mlp_g, wff1, wff2, post_mlp_g)


def kernel(x_prompt, x_sample, meta_tokens, pre_mix_g, w_in, q_norm_g, w_uq, kv_norm_g, w_ukv, conv_w, f_w1, f_b1, f_freq, f_w2, f_b2, f_w3, f_decay, hy_bias, attn_out_g, hy_out_g, w_o, post_mix_g, pre_mlp_g, w_ff1, w_ff2, post_mlp_g):
    assert pre_mix_g.shape[0] == 1, "single-layer trunk"
    w1, wuq, wukv = _prep_weights(w_in[0], w_uq[0], w_ukv[0])
    shared = (meta_tokens, pre_mix_g, w1, q_norm_g, wuq, kv_norm_g, wukv, conv_w[0],
              f_w1[0], f_b1, f_freq, f_w2[0], f_b2, f_w3[0], f_decay, hy_bias[0][:, None, :], attn_out_g, hy_out_g,
              w_o[0].astype(BF16), post_mix_g, pre_mlp_g, w_ff1[0].astype(BF16), w_ff2[0].astype(BF16), post_mlp_g)
    return (_trunk(x_prompt, *shared), _trunk(x_sample, *shared))
```
